```python
import math
import jax, jax.numpy as jnp
from jax import lax
import numpy as np

D_MODEL = 1024
BATCH = 4
SEQ = 4096
DEPTH = 4

GRID_W = 64
CTX_LEN = 256
N_EVEN = (DEPTH + 1) // 2
N_ODD = DEPTH // 2

D_HY = D_MODEL // 2
HY_ORDER = 2
HY_SHORT = 3
HY_EMB_BANDS = 16
HY_EMB = 1 + 2 * HY_EMB_BANDS
HY_FILTER_HIDDEN = 64
HY_DECAY_TARGET = 1e-2
HY_FAST_DECAY = 0.3
HY_SLOW_DECAY = 1.5
HEAD_DIM = 64
N_HEADS = (D_MODEL // 2) // HEAD_DIM
N_KV_HEADS = 2
GROUP = N_HEADS // N_KV_HEADS
WINDOW = 128
ATT_BLOCK = 128
ROPE_BASE = 10000.0
HY_COLS = 3 * D_HY
Q_COLS = N_HEADS * HEAD_DIM
KV_COLS = N_KV_HEADS * HEAD_DIM
IN_COLS = HY_COLS + Q_COLS + 2 * KV_COLS
MIX_OUT = D_HY + Q_COLS
D_CONF = D_MODEL
CONF_KERNEL = 31
N_EXPERTS = 32
TOP_K = 4
D_EXPERT = D_MODEL
SWIGLU_LIMIT = 7.0
SWIGLU_ALPHA = 1.702
MOE_BLOCK = 128
EPS = 1e-6
NEG_INF = -1e30

kernel_name = 'hybrid_hyena_swa_conformer_moe_dit'


def rms_norm(x, g):
    xf = x.astype(jnp.float32)
    y = xf * lax.rsqrt(jnp.mean(xf * xf, axis=-1, keepdims=True) + EPS)
    return (y * g.astype(jnp.float32)).astype(x.dtype)


def layer_norm(x, g, b):
    xf = x.astype(jnp.float32)
    mu = jnp.mean(xf, axis=-1, keepdims=True)
    var = jnp.mean(jnp.square(xf - mu), axis=-1, keepdims=True)
    y = (xf - mu) * lax.rsqrt(var + EPS)
    return (y * g.astype(jnp.float32) + b.astype(jnp.float32)).astype(x.dtype)


def adaln(c_vec, w, b, n):
    m = jax.nn.silu(c_vec) @ w + b
    return [t[:, None, :] for t in jnp.split(m, n, axis=-1)]


def modulate(h, shift, scale):
    return h * (1 + scale) + shift


def depthwise_conv(x, w, b):
    k = w.shape[0]
    pad = (k - 1) // 2
    y = lax.conv_general_dilated(x, w[:, None, :].astype(x.dtype), window_strides=(1,),
                                 padding=[(pad, pad)], dimension_numbers=('NWC', 'WIO', 'NWC'),
                                 feature_group_count=x.shape[-1])
    return y + b


def hyena_filters(L, w1, b1, w2, b2, w3, b3, w_out, freq):
    f32 = jnp.float32
    t = jnp.arange(L, dtype=f32)
    t_norm = t / max(L - 1, 1)
    bands = jnp.linspace(1e-4, HY_EMB_BANDS - 1, HY_EMB_BANDS, dtype=f32)
    ang = (2.0 * math.pi / L) * t[:, None] * bands[None, :]
    feats = jnp.concatenate([t_norm[:, None], jnp.cos(ang), jnp.sin(ang)], axis=-1)
    fr = freq.astype(f32)
    h = jnp.sin(fr * (feats @ w1.astype(f32) + b1.astype(f32)))
    h = jnp.sin(fr * (h @ w2.astype(f32) + b2.astype(f32)))
    h = jnp.sin(fr * (h @ w3.astype(f32) + b3.astype(f32)))
    h = (h @ w_out.astype(f32)).reshape(L, 2, HY_ORDER, D_HY)
    log_t = math.log(HY_DECAY_TARGET)
    deltas = jnp.abs(jnp.linspace(log_t / HY_SLOW_DECAY, log_t / HY_FAST_DECAY, D_HY, dtype=f32))
    h = h * jnp.exp(-t_norm[:, None] * deltas[None, :])[:, None, None, :]
    fwd, bwd = h[:, 0], h[:, 1]
    buf = jnp.concatenate([fwd, jnp.zeros((1, HY_ORDER, D_HY), f32), bwd[:0:-1]], axis=0)
    buf = buf * lax.rsqrt(jnp.sum(buf * buf, axis=0, keepdims=True) + EPS)
    return jnp.fft.rfft(buf, axis=0)


def long_conv(z, h_freq):
    L = z.shape[1]
    zf = jnp.fft.rfft(z.astype(jnp.float32), n=2 * L, axis=1)
    y = jnp.fft.irfft(zf * h_freq[None], n=2 * L, axis=1)[:, :L]
    return y.astype(z.dtype)


def hyena(u, conv_w, conv_b, h_freq, skip):
    u = depthwise_conv(u, conv_w, conv_b)
    v, x1, x2 = jnp.split(u, 3, axis=-1)
    z = v
    for n, gate in enumerate((x1, x2)):
        z = gate * (long_conv(z, h_freq[:, n]) + skip[n] * z)
    return z


def rope_1d(x, pos):
    n = x.shape[-1] // 2
    inv = ROPE_BASE ** (-jnp.arange(n, dtype=jnp.float32) / n)
    ang = pos[:, None] * inv[None, :]
    cos = jnp.cos(ang)[:, None, :].astype(x.dtype)
    sin = jnp.sin(ang)[:, None, :].astype(x.dtype)
    x1, x2 = x[..., :n], x[..., n:]
    return jnp.concatenate([x1 * cos - x2 * sin, x2 * cos + x1 * sin], axis=-1)


def rope_2d(x, row_pos, col_pos):
    half = x.shape[-1] // 2
    return jnp.concatenate([rope_1d(x[..., :half], row_pos), rope_1d(x[..., half:], col_pos)], axis=-1)


def windowed_attention(q, k, v, k_ctx, v_ctx, sink):
    B, L = q.shape[:2]
    nb = L // ATT_BLOCK
    kw_len = 3 * ATT_BLOCK
    qb = q.reshape(B, nb, ATT_BLOCK, N_KV_HEADS, GROUP, HEAD_DIM)
    pad = ((0, 0), (ATT_BLOCK, ATT_BLOCK), (0, 0), (0, 0))
    kp = jnp.pad(k, pad).reshape(B, nb + 2, ATT_BLOCK, N_KV_HEADS, HEAD_DIM)
    vp = jnp.pad(v, pad).reshape(B, nb + 2, ATT_BLOCK, N_KV_HEADS, HEAD_DIM)
    kw = jnp.concatenate([kp[:, :-2], kp[:, 1:-1], kp[:, 2:]], axis=2)
    vw = jnp.concatenate([vp[:, :-2], vp[:, 1:-1], vp[:, 2:]], axis=2)
    r = jnp.arange(ATT_BLOCK)[:, None]
    j = jnp.arange(kw_len)[None, :]
    kpos = jnp.arange(nb)[:, None, None] * ATT_BLOCK - ATT_BLOCK + j[None]
    mask = (jnp.abs(j - ATT_BLOCK - r) <= WINDOW)[None] & (kpos >= 0) & (kpos < L)
    s_win = jnp.einsum('bnqhgd,bnkhd->bnhgqk', qb, kw).astype(jnp.float32)
    s_win = jnp.where(mask[None, :, None, None], s_win, NEG_INF)
    s_ctx = jnp.einsum('bnqhgd,bkhd->bnhgqk', qb, k_ctx).astype(jnp.float32)
    s_sink = jnp.broadcast_to(sink.astype(jnp.float32)[None, None, :, :, None, None], s_win.shape[:-1] + (1,))
    p = jax.nn.softmax(jnp.concatenate([s_win, s_ctx, s_sink], axis=-1), axis=-1)
    n_ctx = k_ctx.shape[1]
    out = (jnp.einsum('bnhgqk,bnkhd->bnqhgd', p[..., :kw_len].astype(v.dtype), vw)
           + jnp.einsum('bnhgqk,bkhd->bnqhgd', p[..., kw_len:kw_len + n_ctx].astype(v.dtype), v_ctx))
    return out.reshape(B, L, N_HEADS * HEAD_DIM)


def context_attention(q, k, v, sink):
    B, C = q.shape[:2]
    qg = q.reshape(B, C, N_KV_HEADS, GROUP, HEAD_DIM)
    s = jnp.einsum('bqhgd,bkhd->bhgqk', qg, k).astype(jnp.float32)
    s_sink = jnp.broadcast_to(sink.astype(jnp.float32)[None, :, :, None, None], s.shape[:-1] + (1,))
    p = jax.nn.softmax(jnp.concatenate([s, s_sink], axis=-1), axis=-1)[..., :-1]
    out = jnp.einsum('bhgqk,bkhd->bqhgd', p.astype(v.dtype), v)
    return out.reshape(B, C, N_HEADS * HEAD_DIM)


def conformer_conv(h, w1, b1, dw_w, dw_b, ln_g, ln_b, w2, b2):
    a = h @ w1 + b1
    a, g = jnp.split(a, 2, axis=-1)
    a = a * jax.nn.sigmoid(g)
    a = depthwise_conv(a, dw_w, dw_b)
    a = jax.nn.silu(layer_norm(a, ln_g, ln_b))
    return a @ w2 + b2


def moe(h, rw, rb, wg, bg, wu, bu, wd, bd):
    B, L, D = h.shape
    T = B * L
    TK = T * TOP_K
    xf = h.reshape(T, D)
    logits = (xf @ rw + rb).astype(jnp.float32)
    top_v, top_i = lax.top_k(logits, TOP_K)
    gates = jax.nn.softmax(top_v, axis=-1)
    eid = top_i.reshape(-1)
    tok = jnp.arange(TK, dtype=jnp.int32) // TOP_K
    gw = gates.reshape(-1)
    order = jnp.argsort(eid)
    se, st, sg = eid[order], tok[order], gw[order]
    counts = jnp.bincount(eid, length=N_EXPERTS)
    starts = jnp.cumsum(counts) - counts
    pcounts = (counts + MOE_BLOCK - 1) // MOE_BLOCK * MOE_BLOCK
    pends = jnp.cumsum(pcounts)
    pstarts = pends - pcounts
    dest = pstarts[se] + (jnp.arange(TK, dtype=jnp.int32) - starts[se])
    n_rows = TK + N_EXPERTS * MOE_BLOCK
    n_blk = n_rows // MOE_BLOCK
    row_tok = jnp.full((n_rows,), T, dtype=jnp.int32).at[dest].set(st)
    row_gate = jnp.zeros((n_rows,), jnp.float32).at[dest].set(sg)
    blk_exp = jnp.minimum(jnp.searchsorted(pends, jnp.arange(n_blk, dtype=jnp.int32) * MOE_BLOCK, side='right'),
                          N_EXPERTS - 1)
    x_rows = jnp.concatenate([xf, jnp.zeros((1, D), xf.dtype)], axis=0)[row_tok].reshape(n_blk, MOE_BLOCK, D)

    def expert_block(args):
        xb, e = args
        g = jnp.minimum(xb @ wg[e] + bg[e], SWIGLU_LIMIT)
        u = jnp.clip(xb @ wu[e] + bu[e], -SWIGLU_LIMIT, SWIGLU_LIMIT)
        a = g * jax.nn.sigmoid(SWIGLU_ALPHA * g) * (u + 1)
        return a @ wd[e] + bd[e]

    y_rows = lax.map(expert_block, (x_rows, blk_exp)).reshape(n_rows, D)
    y = jnp.zeros((T + 1, D), h.dtype).at[row_tok].add(y_rows * row_gate[:, None].astype(h.dtype))
    return y[:T].reshape(B, L, D)


def setup_inputs(seed: int = 0) -> dict:
    key = jax.random.key(seed)
    ks = iter(jax.random.split(key, 64))
    f32 = jnp.float32
    D = D_MODEL

    def nrm(shape, scale):
        return jax.random.normal(next(ks), shape, f32) * scale

    def gain(shape):
        return 1.0 + nrm(shape, 0.05)

    return {
        'x': nrm((BATCH, SEQ, D), 1.0),
        'c': nrm((BATCH, D), 1.0),
        'ctx': nrm((BATCH, CTX_LEN, D), 1.0),
        'c_ctx': nrm((D,), 1.0),
        'mod_w': nrm((DEPTH, D, 6 * D), D ** -0.5),
        'mod_b': nrm((DEPTH, 6 * D), 0.01),
        'norm1_g': gain((DEPTH, D)),
        'norm2_g': gain((DEPTH, D)),
        'ev_w_in': nrm((N_EVEN, D, IN_COLS), D ** -0.5),
        'ev_w_out': nrm((N_EVEN, MIX_OUT, D), MIX_OUT ** -0.5),
        'hy_conv_w': nrm((N_EVEN, HY_SHORT, HY_COLS), HY_SHORT ** -0.5),
        'hy_conv_b': nrm((N_EVEN, HY_COLS), 0.01),
        'hy_w1': nrm((N_EVEN, HY_EMB, HY_FILTER_HIDDEN), HY_EMB ** -0.5),
        'hy_b1': nrm((N_EVEN, HY_FILTER_HIDDEN), 0.1),
        'hy_w2': nrm((N_EVEN, HY_FILTER_HIDDEN, HY_FILTER_HIDDEN), HY_FILTER_HIDDEN ** -0.5),
        'hy_b2': nrm((N_EVEN, HY_FILTER_HIDDEN), 0.1),
        'hy_w3': nrm((N_EVEN, HY_FILTER_HIDDEN, HY_FILTER_HIDDEN), HY_FILTER_HIDDEN ** -0.5),
        'hy_b3': nrm((N_EVEN, HY_FILTER_HIDDEN), 0.1),
        'hy_w_out': nrm((N_EVEN, HY_FILTER_HIDDEN, 2 * HY_ORDER * D_HY), HY_FILTER_HIDDEN ** -0.5),
        'hy_freq': gain((N_EVEN, HY_FILTER_HIDDEN)),
        'hy_skip': nrm((N_EVEN, HY_ORDER, D_HY), 0.5),
        'q_norm_g': gain((N_EVEN, HEAD_DIM)),
        'k_norm_g': gain((N_EVEN, HEAD_DIM)),
        'attn_sink': nrm((N_EVEN, N_HEADS), 0.5),
        'cf_w1': nrm((N_ODD, D, 2 * D_CONF), D ** -0.5),
        'cf_b1': nrm((N_ODD, 2 * D_CONF), 0.01),
        'cf_dw_w': nrm((N_ODD, CONF_KERNEL, D_CONF), CONF_KERNEL ** -0.5),
        'cf_dw_b': nrm((N_ODD, D_CONF), 0.01),
        'cf_ln_g': gain((N_ODD, D_CONF)),
        'cf_ln_b': nrm((N_ODD, D_CONF), 0.01),
        'cf_w2': nrm((N_ODD, D_CONF, D), D_CONF ** -0.5),
        'cf_b2': nrm((N_ODD, D), 0.01),
        'moe_router_w': nrm((DEPTH, D, N_EXPERTS), D ** -0.5),
        'moe_router_b': nrm((DEPTH, N_EXPERTS), 0.01),
        'moe_w_gate': nrm((DEPTH, N_EXPERTS, D, D_EXPERT), D ** -0.5),
        'moe_b_gate': nrm((DEPTH, N_EXPERTS, D_EXPERT), 0.01),
        'moe_w_up': nrm((DEPTH, N_EXPERTS, D, D_EXPERT), D ** -0.5),
        'moe_b_up': nrm((DEPTH, N_EXPERTS, D_EXPERT), 0.01),
        'moe_w_down': nrm((DEPTH, N_EXPERTS, D_EXPERT, D), D_EXPERT ** -0.5),
        'moe_b_down': nrm((DEPTH, N_EXPERTS, D), 0.01),
    }


def reference(x, c, ctx, c_ctx, mod_w, mod_b, norm1_g, norm2_g, ev_w_in, ev_w_out,
              hy_conv_w, hy_conv_b, hy_w1, hy_b1, hy_w2, hy_b2, hy_w3, hy_b3, hy_w_out, hy_freq, hy_skip,
              q_norm_g, k_norm_g, attn_sink,
              cf_w1, cf_b1, cf_dw_w, cf_dw_b, cf_ln_g, cf_ln_b, cf_w2, cf_b2,
              moe_router_w, moe_router_b, moe_w_gate, moe_b_gate, moe_w_up, moe_b_up, moe_w_down, moe_b_down):
    B, L, _ = x.shape
    C = ctx.shape[1]
    ROWS = L // GRID_W
    row_pos = jnp.repeat(jnp.arange(ROWS, dtype=jnp.float32), GRID_W)
    col_pos = jnp.tile(jnp.arange(GRID_W, dtype=jnp.float32), ROWS)
    q_scale = HEAD_DIM ** -0.5
    last_even = (DEPTH - 1) // 2 * 2
    c_con = c_ctx[None, :]
    xl, xc = x, ctx
    for l in range(DEPTH):
        ctx_live = l < last_even
        sh1, sc1, g1, sh2, sc2, g2 = adaln(c, mod_w[l], mod_b[l], 6)
        if ctx_live:
            csh1, csc1, cg1, csh2, csc2, cg2 = adaln(c_con, mod_w[l], mod_b[l], 6)
        if l % 2 == 0:
            e = l // 2
            filt = (hy_w1[e], hy_b1[e], hy_w2[e], hy_b2[e], hy_w3[e], hy_b3[e], hy_w_out[e], hy_freq[e])
            sink = attn_sink[e].reshape(N_KV_HEADS, GROUP)
            h = modulate(rms_norm(xl, norm1_g[l]), sh1, sc1)
            p = h @ ev_w_in[e]
            hy = hyena(p[..., :HY_COLS], hy_conv_w[e], hy_conv_b[e], hyena_filters(L, *filt), hy_skip[e])
            q = p[..., HY_COLS:HY_COLS + Q_COLS].reshape(B, L, N_HEADS, HEAD_DIM)
            k = p[..., HY_COLS + Q_COLS:HY_COLS + Q_COLS + KV_COLS].reshape(B, L, N_KV_HEADS, HEAD_DIM)
            v = p[..., HY_COLS + Q_COLS + KV_COLS:].reshape(B, L, N_KV_HEADS, HEAD_DIM)
            q = rope_2d(rms_norm(q, q_norm_g[e]), row_pos, col_pos) * q_scale
            k = rope_2d(rms_norm(k, k_norm_g[e]), row_pos, col_pos)
            if not ctx_live:
                csh1, csc1 = adaln(c_con, mod_w[l][:, :2 * D_MODEL], mod_b[l][:2 * D_MODEL], 2)
            hc = modulate(rms_norm(xc, norm1_g[l]), csh1, csc1)
            pc = hc @ ev_w_in[e] if ctx_live else hc @ ev_w_in[e][:, HY_COLS + Q_COLS:]
            k_c = rms_norm(pc[..., -2 * KV_COLS:-KV_COLS].reshape(B, C, N_KV_HEADS, HEAD_DIM), k_norm_g[e])
            v_c = pc[..., -KV_COLS:].reshape(B, C, N_KV_HEADS, HEAD_DIM)
            att = windowed_attention(q, k, v, k_c, v_c, sink)
            xl = xl + g1 * (jnp.concatenate([hy, att], axis=-1) @ ev_w_out[e])
            if ctx_live:
                hy_c = hyena(pc[..., :HY_COLS], hy_conv_w[e], hy_conv_b[e], hyena_filters(C, *filt), hy_skip[e])
                q_c = rms_norm(pc[..., HY_COLS:HY_COLS + Q_COLS].reshape(B, C, N_HEADS, HEAD_DIM), q_norm_g[e]) * q_scale
                att_c = context_attention(q_c, k_c, v_c, sink)
                xc = xc + cg1 * (jnp.concatenate([hy_c, att_c], axis=-1) @ ev_w_out[e])
        else:
            o = l // 2
            cf = (cf_w1[o], cf_b1[o], cf_dw_w[o], cf_dw_b[o], cf_ln_g[o], cf_ln_b[o], cf_w2[o], cf_b2[o])
            xl = xl + g1 * conformer_conv(modulate(rms_norm(xl, norm1_g[l]), sh1, sc1), *cf)
            if ctx_live:
                xc = xc + cg1 * conformer_conv(modulate(rms_norm(xc, norm1_g[l]), csh1, csc1), *cf)
        mw = (moe_router_w[l], moe_router_b[l], moe_w_gate[l], moe_b_gate[l], moe_w_up[l], moe_b_up[l],
              moe_w_down[l], moe_b_down[l])
        xl = xl + g2 * moe(modulate(rms_norm(xl, norm2_g[l]), sh2, sc2), *mw)
        if ctx_live:
            xc = xc + cg2 * moe(modulate(rms_norm(xc, norm2_g[l]), csh2, csc2), *mw)
    return xl
```

```python
import functools
import math

import numpy as np
import jax
import jax.numpy as jnp
from jax import lax
from jax.experimental import pallas as pl
from jax.experimental.pallas import tpu as pltpu

F32 = jnp.float32
BF16 = jnp.bfloat16

D = 1024
NB = 4
SEQ = 4096
CTXL = 256
DEPTH = 4
GRID_W = 64
T_LAT = NB * SEQ
T_CTX = NB * CTXL
T_ALL = T_LAT + T_CTX

HY = 512
HYC = 3 * HY
QC = 512
KVC = 128
INC = HYC + QC + 2 * KVC
HEAD = 64
NHEADS = 8
NKV = 2
GROUP = 4
WINDOW = 128
ROPE_BASE = 10000.0
HY_BANDS = 16
HY_HID = 64
CONF_K = 31
NE = 32
TOPK = 4
SWIGLU_LIMIT = 7.0
SWIGLU_ALPHA = 1.702
EPS = 1e-6
NEG = -1e30

LANES = 128
VMEM_LIMIT = 56 * 1024 * 1024

TB = 512
TBR = 256
TBC = 128
BM = 256
TQ = 128


def _cp(sem, vmem=VMEM_LIMIT):
    return pltpu.CompilerParams(dimension_semantics=sem, vmem_limit_bytes=vmem)


def _group(row0):
    return jnp.where(row0 < T_LAT, row0 // SEQ, NB)


def _split(a):
    hi = a.astype(BF16)
    lo = (a - hi.astype(F32)).astype(BF16)
    return hi, lo


def _dot(a, b):
    return jnp.dot(a, b, preferred_element_type=F32)


def _dot3(a, b):
    ah, al = _split(a)
    bh, bl = _split(b)
    return _dot(ah, bh) + _dot(ah, bl) + _dot(al, bh)


def _norm_mod(x, g, sh, sc):
    ms = jnp.mean(x * x, axis=-1, keepdims=True)
    h = x * lax.rsqrt(ms + EPS) * g
    return h * (1.0 + sc) + sh


def _mod_spec(l, chunk):
    return pl.BlockSpec((None, 8, D), lambda i, *_: (l, 0, chunk))


def _adaln_kernel(c_ref, w_ref, b_ref, o_ref):
    c = c_ref[...]
    s = (c * jax.nn.sigmoid(c)).astype(BF16)
    o_ref[...] = _dot(s, w_ref[...].astype(BF16)) + b_ref[...]


def _adaln(cc8, mod_w, mod_b):
    tn = 1536
    return pl.pallas_call(
        _adaln_kernel,
        grid=(DEPTH, 6 * D // tn),
        in_specs=[
            pl.BlockSpec((8, D), lambda l, j: (0, 0)),
            pl.BlockSpec((None, D, tn), lambda l, j: (l, 0, j)),
            pl.BlockSpec((None, 1, tn), lambda l, j: (l, 0, j)),
        ],
        out_specs=pl.BlockSpec((None, 8, tn), lambda l, j: (l, 0, j)),
        out_shape=jax.ShapeDtypeStruct((DEPTH, 8, 6 * D), F32),
        compiler_params=_cp(("arbitrary", "arbitrary")),
        name="adaln",
    )(cc8, mod_w, mod_b.reshape(DEPTH, 1, 6 * D))


def _nmm_kernel(x_ref, g_ref, sh_ref, sc_ref, w_ref, *rest, glu):
    grp = _group(pl.program_id(0) * TB)
    h = _norm_mod(x_ref[...], g_ref[...], sh_ref[pl.ds(grp, 1), :], sc_ref[pl.ds(grp, 1), :])
    acc = _dot(h.astype(BF16), w_ref[...])
    if glu:
        b_ref, o_ref = rest
        acc = acc + b_ref[...]
        n = acc.shape[1] // 2
        o_ref[...] = acc[:, :n] * jax.nn.sigmoid(acc[:, n:])
    else:
        (o_ref,) = rest
        o_ref[...] = acc


def _nmm(x, nrows, g, mods, l, w_bf, bias=None):
    n = w_bf.shape[1]
    glu = bias is not None
    in_specs = [
        pl.BlockSpec((TB, D), lambda i: (i, 0)),
        pl.BlockSpec((1, D), lambda i: (0, 0)),
        _mod_spec(l, 0),
        _mod_spec(l, 1),
        pl.BlockSpec((D, n), lambda i: (0, 0)),
    ]
    args = [x, g.reshape(1, D), mods, mods, w_bf]
    if glu:
        in_specs.append(pl.BlockSpec((1, n), lambda i: (0, 0)))
        args.append(bias.reshape(1, n))
    nout = n // 2 if glu else n
    return pl.pallas_call(
        functools.partial(_nmm_kernel, glu=glu),
        grid=(nrows // TB,),
        in_specs=in_specs,
        out_specs=pl.BlockSpec((TB, nout), lambda i: (i, 0)),
        out_shape=jax.ShapeDtypeStruct((nrows, nout), F32),
        compiler_params=_cp(("arbitrary",)),
        name="norm_mod_matmul",
    )(*args)


def _sconv_kernel(u_ref, w_ref, b_ref, o_ref):
    u = u_ref[...]
    n = u.shape[0]
    r = lax.broadcasted_iota(jnp.int32, u.shape, 0)
    up = jnp.where(r == 0, 0.0, pltpu.roll(u, 1, axis=0))
    un = jnp.where(r == n - 1, 0.0, pltpu.roll(u, n - 1, axis=0))
    o_ref[...] = w_ref[0:1, :] * up + w_ref[1:2, :] * u + w_ref[2:3, :] * un + b_ref[...]


def _sconv(p, ls, blk0, w, b):
    cb = 256
    return pl.pallas_call(
        _sconv_kernel,
        grid=(NB, HYC // cb),
        in_specs=[
            pl.BlockSpec((ls, cb), lambda s, j: (blk0 + s, j)),
            pl.BlockSpec((3, cb), lambda s, j: (0, j)),
            pl.BlockSpec((1, cb), lambda s, j: (0, j)),
        ],
        out_specs=pl.BlockSpec((ls, cb), lambda s, j: (s, j)),
        out_shape=jax.ShapeDtypeStruct((NB * ls, HYC), F32),
        compiler_params=_cp(("arbitrary", "arbitrary")),
        name="hyena_short_conv",
    )(p, w, b.reshape(1, HYC))


def _filt_kernel(f_ref, w1_ref, b1_ref, w2_ref, b2_ref, w3_ref, b3_ref, fr_ref, wf_ref, wb_ref, dl_ref,
                 o_ref, hid_ref, *, ls):
    @pl.when(pl.program_id(0) == 0)
    def _():
        fr = fr_ref[...]
        h = jnp.sin(fr * (_dot3(f_ref[...], w1_ref[...]) + b1_ref[...]))
        h = jnp.sin(fr * (_dot3(h, w2_ref[...]) + b2_ref[...]))
        hid_ref[...] = jnp.sin(fr * (_dot3(h, w3_ref[...]) + b3_ref[...]))

    h = hid_ref[...]
    n = lax.broadcasted_iota(jnp.int32, (h.shape[0], 1), 0)
    buf = jnp.where(n < ls, _dot3(h, wf_ref[...]), _dot3(h, wb_ref[...]))
    buf = buf * jnp.exp(-f_ref[:, 0:1] * dl_ref[...])
    buf = jnp.where(n == ls, 0.0, buf)
    o_ref[...] = buf * lax.rsqrt(jnp.sum(buf * buf, axis=0, keepdims=True) + EPS)


def _filter_feats(ls):
    n = jnp.arange(2 * ls)
    t = jnp.where(n < ls, n, 2 * ls - n).astype(F32)
    t_norm = t / max(ls - 1, 1)
    bands = jnp.linspace(1e-4, HY_BANDS - 1, HY_BANDS, dtype=F32)
    ang = (2.0 * math.pi / ls) * t[:, None] * bands[None, :]
    feats = jnp.concatenate([t_norm[:, None], jnp.cos(ang), jnp.sin(ang)], axis=-1)
    return jnp.pad(feats, ((0, 0), (0, LANES - feats.shape[1])))


def _filters(ls, w1, b1, w2, b2, w3, b3, w_out, freq):
    n = 2 * ls
    cb = 256
    ncol = 2 * HY
    feats = _filter_feats(ls)
    w1p = jnp.pad(w1, ((0, LANES - w1.shape[0]), (0, 0)))
    log_t = math.log(1e-2)
    deltas = jnp.abs(jnp.linspace(log_t / 1.5, log_t / 0.3, HY, dtype=F32))
    dl = jnp.tile(deltas, 2).reshape(1, ncol)
    full = lambda shape: pl.BlockSpec(shape, lambda j: (0,) * len(shape))
    return pl.pallas_call(
        functools.partial(_filt_kernel, ls=ls),
        grid=(ncol // cb,),
        in_specs=[
            full((n, LANES)), full((LANES, HY_HID)), full((1, HY_HID)), full((HY_HID, HY_HID)), full((1, HY_HID)),
            full((HY_HID, HY_HID)), full((1, HY_HID)), full((1, HY_HID)),
            pl.BlockSpec((HY_HID, cb), lambda j: (0, j)),
            pl.BlockSpec((HY_HID, cb), lambda j: (0, ncol // cb + j)),
            pl.BlockSpec((1, cb), lambda j: (0, j)),
        ],
        out_specs=pl.BlockSpec((n, cb), lambda j: (0, j)),
        out_shape=jax.ShapeDtypeStruct((n, ncol), F32),
        scratch_shapes=[pltpu.VMEM((n, HY_HID), F32)],
        compiler_params=_cp(("arbitrary",)),
        name="hyena_filter_mlp",
    )(feats, w1p, b1.reshape(1, -1), w2, b2.reshape(1, -1), w3, b3.reshape(1, -1), freq.reshape(1, -1),
      w_out, w_out, dl)


def _fft_tables(n1, n2):
    n = n1 * n2
    lh = n1 // 2
    k1 = np.arange(n1)
    a = 2.0 * np.pi * ((k1[:, None] * np.arange(lh)[None, :]) % n1) / n1
    c, s = np.cos(a), np.sin(a)
    m1 = np.block([[c, s], [-s, c]])
    af = 2.0 * np.pi * ((k1[:, None] * k1[None, :]) % n1) / n1
    m1f = np.concatenate([np.cos(af), -np.sin(af)], axis=0)
    k2 = np.arange(n2)
    base = ((k2[:, None] * k2[None, :]) % n2) * n1
    th = 2.0 * np.pi * ((base[None] + k1[:, None, None] * k2[None, None, :]) % n) / n
    ct, st = np.cos(th), np.sin(th)
    m3 = np.concatenate([np.concatenate([ct, st], axis=2), np.concatenate([-st, ct], axis=2)], axis=1)
    ctt, stt = np.swapaxes(ct, 1, 2), np.swapaxes(st, 1, 2)
    m3i = np.concatenate([np.concatenate([ctt, -stt], axis=2), np.concatenate([stt, ctt], axis=2)], axis=1)
    ai = 2.0 * np.pi * ((np.arange(lh)[:, None] * k1[None, :]) % n1) / n1
    ci, si = np.cos(ai) / n, np.sin(ai) / n
    m1i = np.block([[ci, -si], [si, ci]])
    to = lambda x: jnp.asarray(x, dtype=F32).astype(BF16)
    return dict(m1=to(m1), m1f=to(m1f), m3=to(m3), m3i=to(m3i), m1i=to(m1i))


def _ffta_kernel(*refs, n1, n2, real_only):
    if real_only:
        f_ref, m_ref, br_ref, bi_ref = refs
    else:
        z_ref, m_ref, br_ref, bi_ref = refs
    lh = n1 // 2
    ls = lh * n2
    m = m_ref[...]

    def body(j, carry):
        if real_only:
            rhs = f_ref[pl.ds(j, n1, stride=n2), :]
        else:
            rhs = jnp.concatenate([z_ref[pl.ds(j, lh, stride=n2), :], z_ref[pl.ds(ls + j, lh, stride=n2), :]], axis=0)
        a = _dot(m, rhs.astype(BF16))
        br_ref[pl.ds(j, n1, stride=n2), :] = a[:n1]
        bi_ref[pl.ds(j, n1, stride=n2), :] = a[n1:]
        return carry

    lax.fori_loop(0, n2, body, 0)


def _fft_a(z, col0, tabs, n1, n2, cb=128):
    n = n1 * n2
    npair = NB // 2
    out = jax.ShapeDtypeStruct((npair, n, HY), F32)
    return pl.pallas_call(
        functools.partial(_ffta_kernel, n1=n1, n2=n2, real_only=False),
        grid=(npair, HY // cb),
        in_specs=[
            pl.BlockSpec((n, cb), lambda p, c: (p, col0 + c)),
            pl.BlockSpec((2 * n1, n1), lambda p, c: (0, 0)),
        ],
        out_specs=[pl.BlockSpec((None, n, cb), lambda p, c: (p, 0, c))] * 2,
        out_shape=[out, out],
        compiler_params=_cp(("arbitrary", "arbitrary")),
        name="fft_stage_a",
    )(z, tabs["m1"])


def _fft_a_filter(buf, tabs, n1, n2, cb=128):
    n = n1 * n2
    ncol = buf.shape[1]
    out = jax.ShapeDtypeStruct((n, ncol), F32)
    return pl.pallas_call(
        functools.partial(_ffta_kernel, n1=n1, n2=n2, real_only=True),
        grid=(ncol // cb,),
        in_specs=[
            pl.BlockSpec((n, cb), lambda c: (0, c)),
            pl.BlockSpec((2 * n1, n1), lambda c: (0, 0)),
        ],
        out_specs=[pl.BlockSpec((n, cb), lambda c: (0, c))] * 2,
        out_shape=[out, out],
        compiler_params=_cp(("arbitrary",)),
        name="fft_stage_a_filter",
    )(buf, tabs["m1f"])


def _fftb_kernel(*refs, n2, k1b, filter_mode):
    if filter_mode:
        br_ref, bi_ref, m3_ref, or_ref, oi_ref = refs
    else:
        br_ref, bi_ref, hr_ref, hi_ref, m3_ref, m3i_ref, or_ref, oi_ref = refs
    for kk in range(k1b):
        rows = slice(kk * n2, (kk + 1) * n2)
        rhs = jnp.concatenate([br_ref[rows, :], bi_ref[rows, :]], axis=0).astype(BF16)
        x = _dot(m3_ref[kk], rhs)
        xr, xi = x[:n2], x[n2:]
        if filter_mode:
            or_ref[rows, :] = xr
            oi_ref[rows, :] = xi
        else:
            hr, hi = hr_ref[rows, :], hi_ref[rows, :]
            y = jnp.concatenate([xr * hr - xi * hi, xr * hi + xi * hr], axis=0).astype(BF16)
            d = _dot(m3i_ref[kk], y)
            or_ref[rows, :] = d[:n2]
            oi_ref[rows, :] = d[n2:]


def _fft_b(br, bi, hr, hi, hcol0, tabs, n1, n2, cb=256, k1b=8):
    npair, n, _ = br.shape
    rb = k1b * n2
    data = pl.BlockSpec((None, rb, cb), lambda k, p, c: (p, k, c))
    filt = pl.BlockSpec((rb, cb), lambda k, p, c: (k, hcol0 + c))
    tab = pl.BlockSpec((k1b, 2 * n2, 2 * n2), lambda k, p, c: (k, 0, 0))
    out = jax.ShapeDtypeStruct((npair, n, HY), F32)
    return pl.pallas_call(
        functools.partial(_fftb_kernel, n2=n2, k1b=k1b, filter_mode=False),
        grid=(n1 // k1b, npair, HY // cb),
        in_specs=[data, data, filt, filt, tab, tab],
        out_specs=[data, data],
        out_shape=[out, out],
        compiler_params=_cp(("arbitrary",) * 3),
        name="fft_stage_b",
    )(br, bi, hr, hi, tabs["m3"], tabs["m3i"])


def _fft_b_filter(br, bi, tabs, n1, n2, cb=256, k1b=8):
    n, ncol = br.shape
    rb = k1b * n2
    data = pl.BlockSpec((rb, cb), lambda k, c: (k, c))
    tab = pl.BlockSpec((k1b, 2 * n2, 2 * n2), lambda k, c: (k, 0, 0))
    out = jax.ShapeDtypeStruct((n, ncol), F32)
    return pl.pallas_call(
        functools.partial(_fftb_kernel, n2=n2, k1b=k1b, filter_mode=True),
        grid=(n1 // k1b, ncol // cb),
        in_specs=[data, data, tab],
        out_specs=[data, data],
        out_shape=[out, out],
        compiler_params=_cp(("arbitrary",) * 2),
        name="fft_stage_b_filter",
    )(br, bi, tabs["m3"])


def _fftc_kernel(dr_ref, di_ref, m_ref, z_ref, x_ref, sk_ref, o_ref, *, n1, n2):
    lh = n1 // 2
    ls = lh * n2
    m = m_ref[...]
    sk = sk_ref[...]

    def body(j, carry):
        rhs = jnp.concatenate([dr_ref[pl.ds(j, n1, stride=n2), :], di_ref[pl.ds(j, n1, stride=n2), :]], axis=0)
        y = _dot(m, rhs.astype(BF16))
        for half in range(2):
            rows = pl.ds(half * ls + j, lh, stride=n2)
            o_ref[rows, :] = x_ref[rows, :] * (y[half * lh:(half + 1) * lh] + sk * z_ref[rows, :])
        return carry

    lax.fori_loop(0, n2, body, 0)


def _fft_c(dr, di, z, zcol0, gate, gcol0, skip_row, tabs, n1, n2, cb=128):
    npair, n, _ = dr.shape
    spec_d = pl.BlockSpec((None, n, cb), lambda p, c: (p, 0, c))
    return pl.pallas_call(
        functools.partial(_fftc_kernel, n1=n1, n2=n2),
        grid=(npair, HY // cb),
        in_specs=[spec_d, spec_d, pl.BlockSpec((n1, 2 * n1), lambda p, c: (0, 0)),
                  pl.BlockSpec((n, cb), lambda p, c: (p, zcol0 + c)),
                  pl.BlockSpec((n, cb), lambda p, c: (p, gcol0 + c)),
                  pl.BlockSpec((1, cb), lambda p, c: (0, c))],
        out_specs=pl.BlockSpec((n, cb), lambda p, c: (p, c)),
        out_shape=jax.ShapeDtypeStruct((npair * n, HY), F32),
        compiler_params=_cp(("arbitrary", "arbitrary")),
        name="fft_stage_c",
    )(dr, di, tabs["m1i"], z, gate, skip_row)


def _rope_tables():
    rows = SEQ // GRID_W
    row_pos = jnp.repeat(jnp.arange(rows, dtype=F32), GRID_W)
    col_pos = jnp.tile(jnp.arange(GRID_W, dtype=F32), rows)
    nf = HEAD // 4
    inv = ROPE_BASE ** (-jnp.arange(nf, dtype=F32) / nf)
    ar = row_pos[:, None] * inv[None, :]
    ac = col_pos[:, None] * inv[None, :]
    cos = jnp.concatenate([jnp.cos(ar), jnp.cos(ar), jnp.cos(ac), jnp.cos(ac)], axis=-1)
    sin = jnp.concatenate([-jnp.sin(ar), jnp.sin(ar), -jnp.sin(ac), jnp.sin(ac)], axis=-1)
    cos = jnp.tile(cos, (1, LANES // HEAD))
    sin = jnp.tile(sin, (1, LANES // HEAD))
    cos = jnp.concatenate([cos, jnp.ones((TBR, LANES), F32)], axis=0)
    sin = jnp.concatenate([sin, jnp.zeros((TBR, LANES), F32)], axis=0)
    return cos, sin


def _head_norm_rope(x, g, bd, cos, sin, scale):
    w = x.shape[1]
    xx = x * x
    hi, lo = _split(xx)
    ms = _dot(hi, bd) + _dot(lo, bd)
    y = x * lax.rsqrt(ms + EPS) * g
    rep = w // LANES
    if rep > 1:
        cos = jnp.concatenate([cos] * rep, axis=1)
        sin = jnp.concatenate([sin] * rep, axis=1)
    lane = lax.broadcasted_iota(jnp.int32, y.shape, 1)
    half = HEAD // 4
    partner = jnp.where(lane % (2 * half) < half, pltpu.roll(y, w - half, axis=1), pltpu.roll(y, half, axis=1))
    return (y * cos + partner * sin) * scale


def _qkprep_kernel(q_ref, k_ref, qg_ref, kg_ref, bdq_ref, bdk_ref, cos_ref, sin_ref, qo_ref, ko_ref):
    cos, sin = cos_ref[...], sin_ref[...]
    qo_ref[...] = _head_norm_rope(q_ref[...], qg_ref[...], bdq_ref[...], cos, sin, HEAD ** -0.5)
    ko_ref[...] = _head_norm_rope(k_ref[...], kg_ref[...], bdk_ref[...], cos, sin, 1.0)


def _block_diag_mean(w):
    i = np.arange(w)
    return jnp.asarray((i[:, None] // HEAD == i[None, :] // HEAD) / HEAD, dtype=F32).astype(BF16)


def _qkprep(p, nrows, qg, kg, cos, sin):
    tb = TBR
    lat_blocks = T_LAT // tb
    seq_blocks = SEQ // tb
    tab = pl.BlockSpec((tb, LANES), lambda i: (jnp.where(i < lat_blocks, i % seq_blocks, seq_blocks), 0))
    return pl.pallas_call(
        _qkprep_kernel,
        grid=(nrows // tb,),
        in_specs=[
            pl.BlockSpec((tb, QC), lambda i: (i, HYC // QC)),
            pl.BlockSpec((tb, KVC), lambda i: (i, (HYC + QC) // KVC)),
            pl.BlockSpec((1, QC), lambda i: (0, 0)),
            pl.BlockSpec((1, KVC), lambda i: (0, 0)),
            pl.BlockSpec((QC, QC), lambda i: (0, 0)),
            pl.BlockSpec((KVC, KVC), lambda i: (0, 0)),
            tab, tab,
        ],
        out_specs=[pl.BlockSpec((tb, QC), lambda i: (i, 0)), pl.BlockSpec((tb, KVC), lambda i: (i, 0))],
        out_shape=[jax.ShapeDtypeStruct((nrows, QC), F32), jax.ShapeDtypeStruct((nrows, KVC), F32)],
        compiler_params=_cp(("arbitrary",)),
        name="qk_norm_rope",
    )(p, p, jnp.tile(qg, NHEADS).reshape(1, QC), jnp.tile(kg, NKV).reshape(1, KVC),
      _block_diag_mean(QC), _block_diag_mean(KVC), cos, sin)


def _attn_kernel(sink_ref, q_ref, kp_ref, kc_ref, kn_ref, vp_ref, vc_ref, vn_ref, kx_ref, vx_ref, o_ref):
    i = pl.program_id(0)
    lat_blocks = T_LAT // TQ
    seq_blocks = SEQ // TQ
    is_lat = i < lat_blocks
    blk = i % seq_blocks
    kwin = jnp.concatenate([kp_ref[...], kc_ref[...], kn_ref[...]], axis=0).astype(BF16)
    vwin = jnp.concatenate([vp_ref[...], vc_ref[...], vn_ref[...]], axis=0).astype(BF16)
    kctx = kx_ref[...].astype(BF16)
    vctx = vx_ref[...].astype(BF16)
    q = q_ref[...].astype(BF16)
    rows = GROUP * TQ
    r = lax.broadcasted_iota(jnp.int32, (rows, 3 * TQ), 0) % TQ
    j = lax.broadcasted_iota(jnp.int32, (rows, 3 * TQ), 1)
    kpos = blk * TQ - TQ + j
    ok = (jnp.abs(j - TQ - r) <= WINDOW) & (kpos >= 0) & (kpos < SEQ) & is_lat
    dn = (((1,), (1,)), ((), ()))
    outs = []
    for h in range(NKV):
        cols = slice(h * HEAD, (h + 1) * HEAD)
        qs = jnp.concatenate([q[:, (GROUP * h + g) * HEAD:(GROUP * h + g + 1) * HEAD] for g in range(GROUP)], axis=0)
        s_w = lax.dot_general(qs, kwin[:, cols], dn, preferred_element_type=F32)
        s_w = jnp.where(ok, s_w, NEG)
        s_c = lax.dot_general(qs, kctx[:, cols], dn, preferred_element_type=F32)
        sk = jnp.concatenate([jnp.full((TQ, 1), sink_ref[GROUP * h + g], F32) for g in range(GROUP)], axis=0)
        m = jnp.maximum(jnp.maximum(jnp.max(s_w, axis=1, keepdims=True), jnp.max(s_c, axis=1, keepdims=True)), sk)
        p_w = jnp.exp(s_w - m)
        p_c = jnp.exp(s_c - m)
        den = jnp.sum(p_w, axis=1, keepdims=True) + jnp.sum(p_c, axis=1, keepdims=True) + jnp.exp(sk - m)
        o = (_dot(p_w.astype(BF16), vwin[:, cols]) + _dot(p_c.astype(BF16), vctx[:, cols])) / den
        outs += [o[g * TQ:(g + 1) * TQ] for g in range(GROUP)]
    o_ref[...] = jnp.concatenate(outs, axis=1)


def _attention(qn, kn, p, nrows, sink):
    nq = nrows // TQ
    lat_blocks = T_LAT // TQ
    seq_blocks = SEQ // TQ
    vcol = (HYC + QC + KVC) // KVC

    def seq_of(i):
        return jnp.where(i < lat_blocks, i // seq_blocks, (i - lat_blocks) // (CTXL // TQ))

    prev = lambda i: jnp.maximum(i - 1, 0)
    nxt = lambda i: jnp.minimum(i + 1, nq - 1)
    ctx_blk = lambda i: T_LAT // CTXL + seq_of(i)
    grid_spec = pltpu.PrefetchScalarGridSpec(
        num_scalar_prefetch=1,
        grid=(nq,),
        in_specs=[
            pl.BlockSpec((TQ, QC), lambda i, s: (i, 0)),
            pl.BlockSpec((TQ, KVC), lambda i, s: (prev(i), 0)),
            pl.BlockSpec((TQ, KVC), lambda i, s: (i, 0)),
            pl.BlockSpec((TQ, KVC), lambda i, s: (nxt(i), 0)),
            pl.BlockSpec((TQ, KVC), lambda i, s: (prev(i), vcol)),
            pl.BlockSpec((TQ, KVC), lambda i, s: (i, vcol)),
            pl.BlockSpec((TQ, KVC), lambda i, s: (nxt(i), vcol)),
            pl.BlockSpec((CTXL, KVC), lambda i, s: (ctx_blk(i), 0)),
            pl.BlockSpec((CTXL, KVC), lambda i, s: (ctx_blk(i), vcol)),
        ],
        out_specs=pl.BlockSpec((TQ, QC), lambda i, s: (i, 0)),
    )
    return pl.pallas_call(
        _attn_kernel,
        grid_spec=grid_spec,
        out_shape=jax.ShapeDtypeStruct((nrows, QC), F32),
        compiler_params=_cp(("arbitrary",)),
        name="windowed_attention",
    )(sink, qn, kn, kn, kn, p, p, p, kn, p)


def _outproj_kernel(*refs, has_ctx):
    if has_ctx:
        x_ref, hl_ref, hc_ref, at_ref, w_ref, g_ref, o_ref = refs
    else:
        x_ref, hl_ref, at_ref, w_ref, g_ref, o_ref = refs
    i = pl.program_id(0)
    grp = _group(i * TB)
    hy = hl_ref[...]
    if has_ctx:
        hy = jnp.where(i < T_LAT // TB, hy, hc_ref[...])
    mix = _dot(hy.astype(BF16), w_ref[:HY, :]) + _dot(at_ref[...].astype(BF16), w_ref[HY:, :])
    o_ref[...] = x_ref[...] + g_ref[pl.ds(grp, 1), :] * mix


def _outproj(x, nrows, hy_l, hy_c, att, w_bf, mods, l):
    has_ctx = hy_c is not None
    lat_blocks = T_LAT // TB
    in_specs = [pl.BlockSpec((TB, D), lambda i: (i, 0)),
                pl.BlockSpec((TB, HY), lambda i: (jnp.minimum(i, lat_blocks - 1), 0))]
    args = [x, hy_l]
    if has_ctx:
        in_specs.append(pl.BlockSpec((TB, HY), lambda i: (jnp.maximum(i - lat_blocks, 0), 0)))
        args.append(hy_c)
    in_specs += [pl.BlockSpec((TB, QC), lambda i: (i, 0)), pl.BlockSpec((HY + QC, D), lambda i: (0, 0)),
                 _mod_spec(l, 2)]
    args += [att, w_bf, mods]
    return pl.pallas_call(
        functools.partial(_outproj_kernel, has_ctx=has_ctx),
        grid=(nrows // TB,),
        in_specs=in_specs,
        out_specs=pl.BlockSpec((TB, D), lambda i: (i, 0)),
        out_shape=jax.ShapeDtypeStruct((nrows, D), F32),
        compiler_params=_cp(("arbitrary",)),
        name="mixer_out_proj",
    )(*args)


HALO = 16


def _conf_kernel(a_ref, ap_ref, an_ref, x_ref, dw_ref, dwb_ref, lg_ref, lb_ref, w_ref, b_ref, g_ref, o_ref):
    tb = TBR
    row0 = pl.program_id(0) * tb
    grp = _group(row0)
    slen = jnp.where(row0 < T_LAT, SEQ, CTXL)
    first = row0 % slen == 0
    last = (row0 + tb) % slen == 0
    ap = jnp.where(first, 0.0, ap_ref[...])
    an = jnp.where(last, 0.0, an_ref[...])
    ext = jnp.concatenate([ap, a_ref[...], an], axis=0)
    pad = (CONF_K - 1) // 2
    acc = jnp.zeros((tb, D), F32)
    for r in range(8):
        er = ext[r:r + tb + HALO + 8]
        for q in range((HALO + 8) // 8 + 1):
            tap = 8 * q + r - (HALO - pad)
            if 0 <= tap < CONF_K and 8 * q + tb <= er.shape[0]:
                acc = acc + er[8 * q:8 * q + tb] * dw_ref[tap:tap + 1, :]
    a = acc + dwb_ref[...]
    mu = jnp.mean(a, axis=-1, keepdims=True)
    var = jnp.mean(jnp.square(a - mu), axis=-1, keepdims=True)
    y = (a - mu) * lax.rsqrt(var + EPS) * lg_ref[...] + lb_ref[...]
    y = y * jax.nn.sigmoid(y)
    out = _dot(y.astype(BF16), w_ref[...]) + b_ref[...]
    o_ref[...] = x_ref[...] + g_ref[pl.ds(grp, 1), :] * out


def _conformer_tail(a, x, nrows, dw_w, dw_b, ln_g, ln_b, w2_bf, b2, mods, l):
    tb = TBR
    hb = tb // HALO
    nhalo = nrows // HALO
    row = lambda v: v.reshape(1, D)
    full = lambda shape: pl.BlockSpec(shape, lambda i: (0,) * len(shape))
    return pl.pallas_call(
        _conf_kernel,
        grid=(nrows // tb,),
        in_specs=[
            pl.BlockSpec((tb, D), lambda i: (i, 0)),
            pl.BlockSpec((HALO, D), lambda i: (jnp.maximum(i * hb - 1, 0), 0)),
            pl.BlockSpec((HALO, D), lambda i: (jnp.minimum((i + 1) * hb, nhalo - 1), 0)),
            pl.BlockSpec((tb, D), lambda i: (i, 0)),
            full((CONF_K, D)), full((1, D)), full((1, D)), full((1, D)), full((D, D)), full((1, D)),
            _mod_spec(l, 2),
        ],
        out_specs=pl.BlockSpec((tb, D), lambda i: (i, 0)),
        out_shape=jax.ShapeDtypeStruct((nrows, D), F32),
        compiler_params=_cp(("arbitrary",)),
        name="conformer_tail",
    )(a, a, a, x, dw_w, row(dw_b), row(ln_g), row(ln_b), w2_bf, row(b2), mods)


def _router_kernel(x_ref, g_ref, sh_ref, sc_ref, rw_ref, rb_ref, tri_ref, h_ref, route_ref, cnt_ref, carry_ref):
    i = pl.program_id(0)

    @pl.when(i == 0)
    def _():
        carry_ref[...] = jnp.zeros_like(carry_ref)

    grp = _group(i * TBR)
    h = _norm_mod(x_ref[...], g_ref[...], sh_ref[pl.ds(grp, 1), :], sc_ref[pl.ds(grp, 1), :])
    h_ref[...] = h
    lg = _dot3(h, rw_ref[...]) + rb_ref[...]
    lane = lax.broadcasted_iota(jnp.int32, lg.shape, 1)
    vals, idxs, hots = [], [], []
    for _ in range(TOPK):
        m = jnp.max(lg, axis=1, keepdims=True)
        idx = jnp.min(jnp.where(lg == m, lane, LANES), axis=1, keepdims=True)
        hot = lane == idx
        vals.append(m)
        idxs.append(idx)
        hots.append(hot)
        lg = jnp.where(hot, -3e38, lg)
    ex = [jnp.exp(v - vals[0]) for v in vals]
    den = ex[0] + ex[1] + ex[2] + ex[3]
    msum = jnp.zeros(lg.shape, F32)
    for hot in hots:
        msum = msum + jnp.where(hot, 1.0, 0.0)
    base = _dot(tri_ref[...], msum.astype(BF16)) + carry_ref[0:1, :]
    route = jnp.zeros(lg.shape, F32)
    for k in range(TOPK):
        pos = jnp.sum(jnp.where(hots[k], base, 0.0), axis=1, keepdims=True)
        route = route + jnp.where(lane == k, idxs[k].astype(F32), 0.0)
        route = route + jnp.where(lane == TOPK + k, ex[k] / den, 0.0)
        route = route + jnp.where(lane == 2 * TOPK + k, pos, 0.0)
    route_ref[...] = route
    carry_ref[0:1, :] = carry_ref[0:1, :] + jnp.sum(msum, axis=0, keepdims=True)
    cnt_ref[...] = carry_ref[...]


def _router(x, nrows, g, mods, l, rw, rb):
    tb = TBR
    rwp = jnp.pad(rw, ((0, 0), (0, LANES - NE)))
    rbp = jnp.concatenate([rb, jnp.full((LANES - NE,), NEG, F32)]).reshape(1, LANES)
    r = np.arange(tb)
    tri = jnp.asarray(r[None, :] < r[:, None], dtype=F32).astype(BF16)
    return pl.pallas_call(
        _router_kernel,
        grid=(nrows // tb,),
        in_specs=[
            pl.BlockSpec((tb, D), lambda i: (i, 0)),
            pl.BlockSpec((1, D), lambda i: (0, 0)),
            _mod_spec(l, 3), _mod_spec(l, 4),
            pl.BlockSpec((D, LANES), lambda i: (0, 0)),
            pl.BlockSpec((1, LANES), lambda i: (0, 0)),
            pl.BlockSpec((tb, tb), lambda i: (0, 0)),
        ],
        out_specs=[
            pl.BlockSpec((tb, D), lambda i: (i, 0)),
            pl.BlockSpec((tb, LANES), lambda i: (i, 0)),
            pl.BlockSpec((8, LANES), lambda i: (0, 0)),
        ],
        out_shape=[
            jax.ShapeDtypeStruct((nrows, D), F32),
            jax.ShapeDtypeStruct((nrows, LANES), F32),
            jax.ShapeDtypeStruct((8, LANES), F32),
        ],
        scratch_shapes=[pltpu.VMEM((8, LANES), F32)],
        compiler_params=_cp(("arbitrary",)),
        name="moe_router",
    )(x, g.reshape(1, D), mods, mods, rwp, rbp, tri)


def _row_copy(src_ref, s, dst_ref, d, sem):
    return pltpu.make_async_copy(src_ref.at[pl.ds(s, 1)], dst_ref.at[pl.ds(d, 1)], sem)


def _dispatch_kernel(dest_ref, h_ref, rows_in_ref, rows_ref, sem):
    del rows_in_ref
    base = pl.program_id(0) * TBR

    def issue(t, carry):
        for k in range(TOPK):
            _row_copy(h_ref, t, rows_ref, dest_ref[(base + t) * TOPK + k], sem).start()
        return carry

    def drain(t, carry):
        for k in range(TOPK):
            _row_copy(h_ref, t, rows_ref, dest_ref[(base + t) * TOPK + k], sem).wait()
        return carry

    lax.fori_loop(0, TBR, issue, 0)
    lax.fori_loop(0, TBR, drain, 0)


def _dispatch(dest, h, nrows, rows0):
    grid_spec = pltpu.PrefetchScalarGridSpec(
        num_scalar_prefetch=1,
        grid=(nrows // TBR,),
        in_specs=[pl.BlockSpec((TBR, D), lambda i, d: (i, 0)), pl.BlockSpec(memory_space=pl.ANY)],
        out_specs=pl.BlockSpec(memory_space=pl.ANY),
        scratch_shapes=[pltpu.SemaphoreType.DMA(())],
    )
    return pl.pallas_call(
        _dispatch_kernel,
        grid_spec=grid_spec,
        out_shape=jax.ShapeDtypeStruct(rows0.shape, F32),
        input_output_aliases={2: 0},
        compiler_params=_cp(("arbitrary",)),
        name="moe_dispatch",
    )(dest, h, rows0)


def _expert_kernel(be_ref, nu_ref, x_ref, wg_ref, bg_ref, wu_ref, bu_ref, wd_ref, bd_ref, o_ref,
                   wgb_ref, wub_ref, wdb_ref):
    j = pl.program_id(0)
    e = be_ref[j]
    prev = be_ref[jnp.maximum(j - 1, 0)]

    @pl.when((j == 0) | (e != prev))
    def _():
        wgb_ref[...] = wg_ref[...].astype(BF16)
        wub_ref[...] = wu_ref[...].astype(BF16)
        wdb_ref[...] = wd_ref[...].astype(BF16)

    @pl.when(j < nu_ref[0])
    def _():
        xb = x_ref[...].astype(BF16)
        g = jnp.minimum(_dot(xb, wgb_ref[...]) + bg_ref[...], SWIGLU_LIMIT)
        u = jnp.clip(_dot(xb, wub_ref[...]) + bu_ref[...], -SWIGLU_LIMIT, SWIGLU_LIMIT)
        a = g * jax.nn.sigmoid(SWIGLU_ALPHA * g) * (u + 1.0)
        o_ref[...] = _dot(a.astype(BF16), wdb_ref[...]) + bd_ref[...]

    @pl.when(j >= nu_ref[0])
    def _():
        o_ref[...] = jnp.zeros_like(o_ref)


def _experts(blk_exp, nused, rows, wg, bg, wu, bu, wd, bd):
    nrows = rows.shape[0]
    wspec = pl.BlockSpec((None, D, D), lambda j, be, nu: (be[j], 0, 0))
    bspec = pl.BlockSpec((None, 1, D), lambda j, be, nu: (be[j], 0, 0))
    grid_spec = pltpu.PrefetchScalarGridSpec(
        num_scalar_prefetch=2,
        grid=(nrows // BM,),
        in_specs=[pl.BlockSpec((BM, D), lambda j, be, nu: (jnp.minimum(j, nu[0] - 1), 0)),
                  wspec, bspec, wspec, bspec, wspec, bspec],
        out_specs=pl.BlockSpec((BM, D), lambda j, be, nu: (j, 0)),
        scratch_shapes=[pltpu.VMEM((D, D), BF16)] * 3,
    )
    b3 = lambda b: b.reshape(NE, 1, D)
    return pl.pallas_call(
        _expert_kernel,
        grid_spec=grid_spec,
        out_shape=jax.ShapeDtypeStruct((nrows, D), F32),
        compiler_params=_cp(("arbitrary",)),
        name="moe_experts",
    )(blk_exp, nused, rows, wg, b3(bg), wu, b3(bu), wd, b3(bd))


def _combine_kernel(dest_ref, x_ref, route_ref, g_ref, y_ref, o_ref, buf_ref, sem):
    base = pl.program_id(0) * TBC
    grp = _group(base)

    def issue(t, carry):
        for k in range(TOPK):
            _row_copy(y_ref, dest_ref[(base + t) * TOPK + k], buf_ref.at[k], t, sem).start()
        return carry

    def drain(t, carry):
        for k in range(TOPK):
            _row_copy(y_ref, dest_ref[(base + t) * TOPK + k], buf_ref.at[k], t, sem).wait()
        return carry

    lax.fori_loop(0, TBC, issue, 0)
    lax.fori_loop(0, TBC, drain, 0)
    acc = jnp.zeros((TBC, D), F32)
    for k in range(TOPK):
        acc = acc + route_ref[:, TOPK + k:TOPK + k + 1] * buf_ref[k]
    o_ref[...] = x_ref[...] + g_ref[pl.ds(grp, 1), :] * acc


def _combine(dest, x, nrows, route, mods, l, y_rows):
    grid_spec = pltpu.PrefetchScalarGridSpec(
        num_scalar_prefetch=1,
        grid=(nrows // TBC,),
        in_specs=[pl.BlockSpec((TBC, D), lambda i, d: (i, 0)),
                  pl.BlockSpec((TBC, LANES), lambda i, d: (i, 0)),
                  _mod_spec(l, 5),
                  pl.BlockSpec(memory_space=pl.ANY)],
        out_specs=pl.BlockSpec((TBC, D), lambda i, d: (i, 0)),
        scratch_shapes=[pltpu.VMEM((TOPK, TBC, D), F32), pltpu.SemaphoreType.DMA(())],
    )
    return pl.pallas_call(
        _combine_kernel,
        grid_spec=grid_spec,
        out_shape=jax.ShapeDtypeStruct((nrows, D), F32),
        compiler_params=_cp(("arbitrary",)),
        name="moe_combine",
    )(dest, x, route, mods, y_rows)


def _moe(x, nrows, g, mods, l, rw, rb, wg, bg, wu, bu, wd, bd):
    h, route, cnt = _router(x, nrows, g, mods, l, rw, rb)
    counts = cnt[0, :NE].astype(jnp.int32)
    pcounts = (counts + BM - 1) // BM * BM
    pends = jnp.cumsum(pcounts)
    pstarts = pends - pcounts
    top_i = route[:, :TOPK].astype(jnp.int32)
    pos = route[:, 2 * TOPK:3 * TOPK].astype(jnp.int32)
    start_of = jnp.sum(jnp.where(top_i[..., None] == jnp.arange(NE), pstarts, 0), axis=-1)
    dest = (start_of + pos).reshape(-1)
    n_rows = nrows * TOPK + NE * BM
    n_blk = n_rows // BM
    nused = pends[-1] // BM
    blk = jnp.minimum(jnp.arange(n_blk), nused - 1) * BM
    blk_exp = jnp.minimum(jnp.sum(blk[:, None] >= pends[None, :], axis=1), NE - 1).astype(jnp.int32)
    rows = _dispatch(dest, h, nrows, jnp.zeros((n_rows, D), F32))
    y_rows = _experts(blk_exp, nused.reshape(1).astype(jnp.int32), rows, wg, bg, wu, bu, wd, bd)
    return _combine(dest, x, nrows, route, mods, l, y_rows)


def _hyena(u, hr, hi, skip, tabs, n1, n2):
    nc = HY // 128
    br, bi = _fft_a(u, 0, tabs, n1, n2)
    dr, di = _fft_b(br, bi, hr, hi, 0, tabs, n1, n2)
    z1 = _fft_c(dr, di, u, 0, u, nc, skip[0:1], tabs, n1, n2)
    br, bi = _fft_a(z1, 0, tabs, n1, n2)
    dr, di = _fft_b(br, bi, hr, hi, HY // 256, tabs, n1, n2)
    return _fft_c(dr, di, z1, 0, u, 2 * nc, skip[1:2], tabs, n1, n2)


def _filter_spectra(ls, filt, tabs, n1, n2):
    buf = _filters(ls, *filt)
    fr, fi = _fft_a_filter(buf, tabs, n1, n2)
    return _fft_b_filter(fr, fi, tabs, n1, n2)


FFT_LAT = (128, 64)
FFT_CTX = (32, 16)


def kernel(x, c, ctx, c_ctx, mod_w, mod_b, norm1_g, norm2_g, ev_w_in, ev_w_out, hy_conv_w, hy_conv_b, hy_w1, hy_b1, hy_w2, hy_b2, hy_w3, hy_b3, hy_w_out, hy_freq, hy_skip, q_norm_g, k_norm_g, attn_sink, cf_w1, cf_b1, cf_dw_w, cf_dw_b, cf_ln_g, cf_ln_b, cf_w2, cf_b2, moe_router_w, moe_router_b, moe_w_gate, moe_b_gate, moe_w_up, moe_b_up, moe_w_down, moe_b_down):
    cc8 = jnp.concatenate([c, c_ctx[None, :], jnp.zeros((8 - NB - 1, D), F32)], axis=0)
    mods = _adaln(cc8, mod_w, mod_b)
    xs = jnp.concatenate([x.reshape(T_LAT, D), ctx.reshape(T_CTX, D)], axis=0)
    cos, sin = _rope_tables()
    tabs_l = _fft_tables(*FFT_LAT)
    tabs_c = _fft_tables(*FFT_CTX)
    nrows = T_ALL
    for l in range(DEPTH):
        ctx_live = l < (DEPTH - 1) // 2 * 2
        if l % 2 == 0:
            e = l // 2
            filt = (hy_w1[e], hy_b1[e], hy_w2[e], hy_b2[e], hy_w3[e], hy_b3[e], hy_w_out[e], hy_freq[e])
            p = _nmm(xs, T_ALL, norm1_g[l], mods, l, ev_w_in[e].astype(BF16))
            hr, hi = _filter_spectra(SEQ, filt, tabs_l, *FFT_LAT)
            u = _sconv(p, SEQ, 0, hy_conv_w[e], hy_conv_b[e])
            hy_l = _hyena(u, hr, hi, hy_skip[e], tabs_l, *FFT_LAT)
            hy_c = None
            if ctx_live:
                hrc, hic = _filter_spectra(CTXL, filt, tabs_c, *FFT_CTX)
                uc = _sconv(p, CTXL, T_LAT // CTXL, hy_conv_w[e], hy_conv_b[e])
                hy_c = _hyena(uc, hrc, hic, hy_skip[e], tabs_c, *FFT_CTX)
            qn, kn = _qkprep(p, T_ALL, q_norm_g[e], k_norm_g[e], cos, sin)
            nrows = T_ALL if ctx_live else T_LAT
            att = _attention(qn, kn, p, nrows, attn_sink[e])
            xs = _outproj(xs, nrows, hy_l, hy_c, att, ev_w_out[e].astype(BF16), mods, l)
        else:
            o = l // 2
            a = _nmm(xs, nrows, norm1_g[l], mods, l, cf_w1[o].astype(BF16), cf_b1[o])
            xs = _conformer_tail(a, xs, nrows, cf_dw_w[o], cf_dw_b[o], cf_ln_g[o], cf_ln_b[o],
                                 cf_w2[o].astype(BF16), cf_b2[o], mods, l)
        xs = _moe(xs, nrows, norm2_g[l], mods, l, moe_router_w[l], moe_router_b[l], moe_w_gate[l], moe_b_gate[l],
                  moe_w_up[l], moe_b_up[l], moe_w_down[l], moe_b_down[l])
    return xs[:T_LAT].reshape(NB, SEQ, D)
```

```python
import functools
import math

import numpy as np
import jax
import jax.numpy as jnp
from jax import lax
from jax.experimental import pallas as pl
from jax.experimental.pallas import tpu as pltpu

F32 = jnp.float32
BF16 = jnp.bfloat16

D = 1024
NB = 4
SEQ = 4096
CTXL = 256
DEPTH = 4
GRID_W = 64
T_LAT = NB * SEQ
T_CTX = NB * CTXL
T_ALL = T_LAT + T_CTX

HY = 512
HYC = 3 * HY
QC = 512
KVC = 128
INC = HYC + QC + 2 * KVC
HEAD = 64
NHEADS = 8
NKV = 2
GROUP = 4
WINDOW = 128
ROPE_BASE = 10000.0
HY_BANDS = 16
HY_HID = 64
CONF_K = 31
NE = 32
TOPK = 4
SWIGLU_LIMIT = 7.0
SWIGLU_ALPHA = 1.702
EPS = 1e-6
NEG = -1e30

LANES = 128
VMEM_LIMIT = 56 * 1024 * 1024

TB = 512
TBR = 256
TBC = 128
BM = 256
TQ = 128


def _cp(sem, vmem=VMEM_LIMIT):
    return pltpu.CompilerParams(dimension_semantics=sem, vmem_limit_bytes=vmem)


def _group(row0):
    return jnp.where(row0 < T_LAT, row0 // SEQ, NB)


def _split(a):
    hi = a.astype(BF16)
    lo = (a - hi.astype(F32)).astype(BF16)
    return hi, lo


def _dot(a, b):
    return jnp.dot(a, b, preferred_element_type=F32)


def _dot3(a, b):
    ah, al = _split(a)
    bh, bl = _split(b)
    return _dot(ah, bh) + _dot(ah, bl) + _dot(al, bh)


def _norm_mod(x, g, sh, sc):
    ms = jnp.mean(x * x, axis=-1, keepdims=True)
    h = x * lax.rsqrt(ms + EPS) * g
    return h * (1.0 + sc) + sh


def _mod_spec(l, chunk):
    return pl.BlockSpec((None, 8, D), lambda i, *_: (l, 0, chunk))


def _adaln_kernel(c_ref, w_ref, b_ref, o_ref):
    c = c_ref[...]
    s = (c * jax.nn.sigmoid(c)).astype(BF16)
    o_ref[...] = _dot(s, w_ref[...].astype(BF16)) + b_ref[...]


def _adaln(cc8, mod_w, mod_b):
    tn = 1536
    return pl.pallas_call(
        _adaln_kernel,
        grid=(DEPTH, 6 * D // tn),
        in_specs=[
            pl.BlockSpec((8, D), lambda l, j: (0, 0)),
            pl.BlockSpec((None, D, tn), lambda l, j: (l, 0, j)),
            pl.BlockSpec((None, 1, tn), lambda l, j: (l, 0, j)),
        ],
        out_specs=pl.BlockSpec((None, 8, tn), lambda l, j: (l, 0, j)),
        out_shape=jax.ShapeDtypeStruct((DEPTH, 8, 6 * D), F32),
        compiler_params=_cp(("arbitrary", "arbitrary")),
        name="adaln",
    )(cc8, mod_w, mod_b.reshape(DEPTH, 1, 6 * D))


def _nmm_kernel(x_ref, g_ref, sh_ref, sc_ref, w_ref, *rest, glu):
    grp = _group(pl.program_id(0) * TB)
    h = _norm_mod(x_ref[...], g_ref[...], sh_ref[pl.ds(grp, 1), :], sc_ref[pl.ds(grp, 1), :])
    acc = _dot(h.astype(BF16), w_ref[...])
    if glu:
        b_ref, o_ref = rest
        acc = acc + b_ref[...]
        n = acc.shape[1] // 2
        o_ref[...] = acc[:, :n] * jax.nn.sigmoid(acc[:, n:])
    else:
        (o_ref,) = rest
        o_ref[...] = acc


def _nmm(x, nrows, g, mods, l, w_bf, bias=None):
    n = w_bf.shape[1]
    glu = bias is not None
    in_specs = [
        pl.BlockSpec((TB, D), lambda i: (i, 0)),
        pl.BlockSpec((1, D), lambda i: (0, 0)),
        _mod_spec(l, 0),
        _mod_spec(l, 1),
        pl.BlockSpec((D, n), lambda i: (0, 0)),
    ]
    args = [x, g.reshape(1, D), mods, mods, w_bf]
    if glu:
        in_specs.append(pl.BlockSpec((1, n), lambda i: (0, 0)))
        args.append(bias.reshape(1, n))
    nout = n // 2 if glu else n
    return pl.pallas_call(
        functools.partial(_nmm_kernel, glu=glu),
        grid=(nrows // TB,),
        in_specs=in_specs,
        out_specs=pl.BlockSpec((TB, nout), lambda i: (i, 0)),
        out_shape=jax.ShapeDtypeStruct((nrows, nout), F32),
        compiler_params=_cp(("arbitrary",)),
        name="norm_mod_matmul",
    )(*args)


def _sconv_kernel(u_ref, w_ref, b_ref, o_ref):
    u = u_ref[...]
    n = u.shape[0]
    r = lax.broadcasted_iota(jnp.int32, u.shape, 0)
    up = jnp.where(r == 0, 0.0, pltpu.roll(u, 1, axis=0))
    un = jnp.where(r == n - 1, 0.0, pltpu.roll(u, n - 1, axis=0))
    o_ref[...] = w_ref[0:1, :] * up + w_ref[1:2, :] * u + w_ref[2:3, :] * un + b_ref[...]


def _sconv(p, ls, blk0, w, b):
    cb = 256
    return pl.pallas_call(
        _sconv_kernel,
        grid=(NB, HYC // cb),
        in_specs=[
            pl.BlockSpec((ls, cb), lambda s, j: (blk0 + s, j)),
            pl.BlockSpec((3, cb), lambda s, j: (0, j)),
            pl.BlockSpec((1, cb), lambda s, j: (0, j)),
        ],
        out_specs=pl.BlockSpec((ls, cb), lambda s, j: (s, j)),
        out_shape=jax.ShapeDtypeStruct((NB * ls, HYC), F32),
        compiler_params=_cp(("arbitrary", "arbitrary")),
        name="hyena_short_conv",
    )(p, w, b.reshape(1, HYC))


def _filt_kernel(f_ref, w1_ref, b1_ref, w2_ref, b2_ref, w3_ref, b3_ref, fr_ref, wf_ref, wb_ref, dl_ref,
                 o_ref, hid_ref, *, ls):
    @pl.when(pl.program_id(0) == 0)
    def _():
        fr = fr_ref[...]
        h = jnp.sin(fr * (_dot3(f_ref[...], w1_ref[...]) + b1_ref[...]))
        h = jnp.sin(fr * (_dot3(h, w2_ref[...]) + b2_ref[...]))
        hid_ref[...] = jnp.sin(fr * (_dot3(h, w3_ref[...]) + b3_ref[...]))

    h = hid_ref[...]
    n = lax.broadcasted_iota(jnp.int32, (h.shape[0], 1), 0)
    buf = jnp.concatenate([_dot3(h[:ls], wf_ref[...]), _dot3(h[ls:], wb_ref[...])], axis=0)
    buf = buf * jnp.exp(-f_ref[:, 0:1] * dl_ref[...])
    buf = jnp.where(n == ls, 0.0, buf)
    o_ref[...] = buf * lax.rsqrt(jnp.sum(buf * buf, axis=0, keepdims=True) + EPS)


def _filter_feats(ls):
    n = jnp.arange(2 * ls)
    t = jnp.where(n < ls, n, 2 * ls - n).astype(F32)
    t_norm = t / max(ls - 1, 1)
    bands = jnp.linspace(1e-4, HY_BANDS - 1, HY_BANDS, dtype=F32)
    ang = (2.0 * math.pi / ls) * t[:, None] * bands[None, :]
    feats = jnp.concatenate([t_norm[:, None], jnp.cos(ang), jnp.sin(ang)], axis=-1)
    return jnp.pad(feats, ((0, 0), (0, LANES - feats.shape[1])))


def _filters(ls, w1, b1, w2, b2, w3, b3, w_out, freq):
    n = 2 * ls
    cb = 256
    ncol = 2 * HY
    feats = _filter_feats(ls)
    w1p = jnp.pad(w1, ((0, LANES - w1.shape[0]), (0, 0)))
    log_t = math.log(1e-2)
    deltas = jnp.abs(jnp.linspace(log_t / 1.5, log_t / 0.3, HY, dtype=F32))
    dl = jnp.tile(deltas, 2).reshape(1, ncol)
    full = lambda shape: pl.BlockSpec(shape, lambda j: (0,) * len(shape))
    return pl.pallas_call(
        functools.partial(_filt_kernel, ls=ls),
        grid=(ncol // cb,),
        in_specs=[
            full((n, LANES)), full((LANES, HY_HID)), full((1, HY_HID)), full((HY_HID, HY_HID)), full((1, HY_HID)),
            full((HY_HID, HY_HID)), full((1, HY_HID)), full((1, HY_HID)),
            pl.BlockSpec((HY_HID, cb), lambda j: (0, j)),
            pl.BlockSpec((HY_HID, cb), lambda j: (0, ncol // cb + j)),
            pl.BlockSpec((1, cb), lambda j: (0, j)),
        ],
        out_specs=pl.BlockSpec((n, cb), lambda j: (0, j)),
        out_shape=jax.ShapeDtypeStruct((n, ncol), F32),
        scratch_shapes=[pltpu.VMEM((n, HY_HID), F32)],
        compiler_params=_cp(("arbitrary",)),
        name="hyena_filter_mlp",
    )(feats, w1p, b1.reshape(1, -1), w2, b2.reshape(1, -1), w3, b3.reshape(1, -1), freq.reshape(1, -1),
      w_out, w_out, dl)


def _fft_tables(n1, n2):
    n = n1 * n2
    lh = n1 // 2
    k1 = np.arange(n1)
    a = 2.0 * np.pi * ((k1[:, None] * np.arange(lh)[None, :]) % n1) / n1
    c, s = np.cos(a), np.sin(a)
    m1 = np.block([[c, s], [-s, c]])
    af = 2.0 * np.pi * ((k1[:, None] * k1[None, :]) % n1) / n1
    m1f = np.concatenate([np.cos(af), -np.sin(af)], axis=0)
    k2 = np.arange(n2)
    base = ((k2[:, None] * k2[None, :]) % n2) * n1
    th = 2.0 * np.pi * ((base[None] + k1[:, None, None] * k2[None, None, :]) % n) / n
    ct, st = np.cos(th), np.sin(th)
    m3 = np.concatenate([np.concatenate([ct, st], axis=2), np.concatenate([-st, ct], axis=2)], axis=1)
    ctt, stt = np.swapaxes(ct, 1, 2), np.swapaxes(st, 1, 2)
    m3i = np.concatenate([np.concatenate([ctt, -stt], axis=2), np.concatenate([stt, ctt], axis=2)], axis=1)
    ai = 2.0 * np.pi * ((np.arange(lh)[:, None] * k1[None, :]) % n1) / n1
    ci, si = np.cos(ai) / n, np.sin(ai) / n
    m1i = np.block([[ci, -si], [si, ci]])
    to = lambda x: jnp.asarray(x, dtype=F32).astype(BF16)
    return dict(m1=to(m1), m1f=to(m1f), m3=to(m3), m3i=to(m3i), m1i=to(m1i))


def _ffta_kernel(*refs, n1, n2, real_only):
    if real_only:
        f_ref, m_ref, br_ref, bi_ref = refs
    else:
        z_ref, m_ref, br_ref, bi_ref = refs
    lh = n1 // 2
    ls = lh * n2
    m = m_ref[...]

    def body(j, carry):
        if real_only:
            rhs = f_ref[pl.ds(j, n1, stride=n2), :]
        else:
            rhs = jnp.concatenate([z_ref[pl.ds(j, lh, stride=n2), :], z_ref[pl.ds(ls + j, lh, stride=n2), :]], axis=0)
        a = _dot(m, rhs.astype(BF16))
        br_ref[pl.ds(j, n1, stride=n2), :] = a[:n1]
        bi_ref[pl.ds(j, n1, stride=n2), :] = a[n1:]
        return carry

    lax.fori_loop(0, n2, body, 0, unroll=2)


def _fft_a(z, col0, tabs, n1, n2, cb=128):
    n = n1 * n2
    npair = NB // 2
    out = jax.ShapeDtypeStruct((npair, n, HY), F32)
    return pl.pallas_call(
        functools.partial(_ffta_kernel, n1=n1, n2=n2, real_only=False),
        grid=(npair, HY // cb),
        in_specs=[
            pl.BlockSpec((n, cb), lambda p, c: (p, col0 + c)),
            pl.BlockSpec((2 * n1, n1), lambda p, c: (0, 0)),
        ],
        out_specs=[pl.BlockSpec((None, n, cb), lambda p, c: (p, 0, c))] * 2,
        out_shape=[out, out],
        compiler_params=_cp(("arbitrary", "arbitrary")),
        name="fft_stage_a",
    )(z, tabs["m1"])


def _fft_a_filter(buf, tabs, n1, n2, cb=128):
    n = n1 * n2
    ncol = buf.shape[1]
    out = jax.ShapeDtypeStruct((n, ncol), F32)
    return pl.pallas_call(
        functools.partial(_ffta_kernel, n1=n1, n2=n2, real_only=True),
        grid=(ncol // cb,),
        in_specs=[
            pl.BlockSpec((n, cb), lambda c: (0, c)),
            pl.BlockSpec((2 * n1, n1), lambda c: (0, 0)),
        ],
        out_specs=[pl.BlockSpec((n, cb), lambda c: (0, c))] * 2,
        out_shape=[out, out],
        compiler_params=_cp(("arbitrary",)),
        name="fft_stage_a_filter",
    )(buf, tabs["m1f"])


def _fftb_kernel(*refs, n2, k1b, filter_mode):
    if filter_mode:
        br_ref, bi_ref, m3_ref, or_ref, oi_ref = refs
    else:
        br_ref, bi_ref, hr_ref, hi_ref, m3_ref, m3i_ref, or_ref, oi_ref = refs
    for kk in range(k1b):
        rows = slice(kk * n2, (kk + 1) * n2)
        rhs = jnp.concatenate([br_ref[rows, :], bi_ref[rows, :]], axis=0).astype(BF16)
        x = _dot(m3_ref[kk], rhs)
        xr, xi = x[:n2], x[n2:]
        if filter_mode:
            or_ref[rows, :] = xr
            oi_ref[rows, :] = xi
        else:
            hr, hi = hr_ref[rows, :], hi_ref[rows, :]
            y = jnp.concatenate([xr * hr - xi * hi, xr * hi + xi * hr], axis=0).astype(BF16)
            d = _dot(m3i_ref[kk], y)
            or_ref[rows, :] = d[:n2]
            oi_ref[rows, :] = d[n2:]


def _fft_b(br, bi, hr, hi, hcol0, tabs, n1, n2, cb=256, k1b=8):
    npair, n, _ = br.shape
    rb = k1b * n2
    data = pl.BlockSpec((None, rb, cb), lambda k, p, c: (p, k, c))
    filt = pl.BlockSpec((rb, cb), lambda k, p, c: (k, hcol0 + c))
    tab = pl.BlockSpec((k1b, 2 * n2, 2 * n2), lambda k, p, c: (k, 0, 0))
    out = jax.ShapeDtypeStruct((npair, n, HY), F32)
    return pl.pallas_call(
        functools.partial(_fftb_kernel, n2=n2, k1b=k1b, filter_mode=False),
        grid=(n1 // k1b, npair, HY // cb),
        in_specs=[data, data, filt, filt, tab, tab],
        out_specs=[data, data],
        out_shape=[out, out],
        compiler_params=_cp(("arbitrary",) * 3),
        name="fft_stage_b",
    )(br, bi, hr, hi, tabs["m3"], tabs["m3i"])


def _fft_b_filter(br, bi, tabs, n1, n2, cb=256, k1b=8):
    n, ncol = br.shape
    rb = k1b * n2
    data = pl.BlockSpec((rb, cb), lambda k, c: (k, c))
    tab = pl.BlockSpec((k1b, 2 * n2, 2 * n2), lambda k, c: (k, 0, 0))
    out = jax.ShapeDtypeStruct((n, ncol), F32)
    return pl.pallas_call(
        functools.partial(_fftb_kernel, n2=n2, k1b=k1b, filter_mode=True),
        grid=(n1 // k1b, ncol // cb),
        in_specs=[data, data, tab],
        out_specs=[data, data],
        out_shape=[out, out],
        compiler_params=_cp(("arbitrary",) * 2),
        name="fft_stage_b_filter",
    )(br, bi, tabs["m3"])


def _fftc_kernel(dr_ref, di_ref, m_ref, z_ref, x_ref, sk_ref, o_ref, *, n1, n2):
    lh = n1 // 2
    ls = lh * n2
    m = m_ref[...]
    sk = sk_ref[...]

    def body(j, carry):
        rhs = jnp.concatenate([dr_ref[pl.ds(j, n1, stride=n2), :], di_ref[pl.ds(j, n1, stride=n2), :]], axis=0)
        y = _dot(m, rhs.astype(BF16))
        for half in range(2):
            rows = pl.ds(half * ls + j, lh, stride=n2)
            o_ref[rows, :] = x_ref[rows, :] * (y[half * lh:(half + 1) * lh] + sk * z_ref[rows, :])
        return carry

    lax.fori_loop(0, n2, body, 0, unroll=2)


def _fft_c(dr, di, z, zcol0, gate, gcol0, skip_row, tabs, n1, n2, cb=128):
    npair, n, _ = dr.shape
    spec_d = pl.BlockSpec((None, n, cb), lambda p, c: (p, 0, c))
    return pl.pallas_call(
        functools.partial(_fftc_kernel, n1=n1, n2=n2),
        grid=(npair, HY // cb),
        in_specs=[spec_d, spec_d, pl.BlockSpec((n1, 2 * n1), lambda p, c: (0, 0)),
                  pl.BlockSpec((n, cb), lambda p, c: (p, zcol0 + c)),
                  pl.BlockSpec((n, cb), lambda p, c: (p, gcol0 + c)),
                  pl.BlockSpec((1, cb), lambda p, c: (0, c))],
        out_specs=pl.BlockSpec((n, cb), lambda p, c: (p, c)),
        out_shape=jax.ShapeDtypeStruct((npair * n, HY), F32),
        compiler_params=_cp(("arbitrary", "arbitrary")),
        name="fft_stage_c",
    )(dr, di, tabs["m1i"], z, gate, skip_row)


def _rope_tables():
    rows = SEQ // GRID_W
    row_pos = jnp.repeat(jnp.arange(rows, dtype=F32), GRID_W)
    col_pos = jnp.tile(jnp.arange(GRID_W, dtype=F32), rows)
    nf = HEAD // 4
    inv = ROPE_BASE ** (-jnp.arange(nf, dtype=F32) / nf)
    ar = row_pos[:, None] * inv[None, :]
    ac = col_pos[:, None] * inv[None, :]
    cos = jnp.concatenate([jnp.cos(ar), jnp.cos(ar), jnp.cos(ac), jnp.cos(ac)], axis=-1)
    sin = jnp.concatenate([-jnp.sin(ar), jnp.sin(ar), -jnp.sin(ac), jnp.sin(ac)], axis=-1)
    cos = jnp.tile(cos, (1, LANES // HEAD))
    sin = jnp.tile(sin, (1, LANES // HEAD))
    cos = jnp.concatenate([cos, jnp.ones((TBR, LANES), F32)], axis=0)
    sin = jnp.concatenate([sin, jnp.zeros((TBR, LANES), F32)], axis=0)
    return cos, sin


def _head_norm_rope(x, g, bd, cos, sin, scale):
    w = x.shape[1]
    xx = x * x
    hi, lo = _split(xx)
    ms = _dot(hi, bd) + _dot(lo, bd)
    y = x * lax.rsqrt(ms + EPS) * g
    rep = w // LANES
    if rep > 1:
        cos = jnp.concatenate([cos] * rep, axis=1)
        sin = jnp.concatenate([sin] * rep, axis=1)
    lane = lax.broadcasted_iota(jnp.int32, y.shape, 1)
    half = HEAD // 4
    partner = jnp.where(lane % (2 * half) < half, pltpu.roll(y, w - half, axis=1), pltpu.roll(y, half, axis=1))
    return (y * cos + partner * sin) * scale


def _qkprep_kernel(q_ref, k_ref, qg_ref, kg_ref, bdq_ref, bdk_ref, cos_ref, sin_ref, qo_ref, ko_ref):
    cos, sin = cos_ref[...], sin_ref[...]
    qo_ref[...] = _head_norm_rope(q_ref[...], qg_ref[...], bdq_ref[...], cos, sin, HEAD ** -0.5)
    ko_ref[...] = _head_norm_rope(k_ref[...], kg_ref[...], bdk_ref[...], cos, sin, 1.0)


def _block_diag_mean(w):
    i = np.arange(w)
    return jnp.asarray((i[:, None] // HEAD == i[None, :] // HEAD) / HEAD, dtype=F32).astype(BF16)


def _qkprep(p, nrows, qg, kg, cos, sin):
    tb = TBR
    lat_blocks = T_LAT // tb
    seq_blocks = SEQ // tb
    tab = pl.BlockSpec((tb, LANES), lambda i: (jnp.where(i < lat_blocks, i % seq_blocks, seq_blocks), 0))
    return pl.pallas_call(
        _qkprep_kernel,
        grid=(nrows // tb,),
        in_specs=[
            pl.BlockSpec((tb, QC), lambda i: (i, HYC // QC)),
            pl.BlockSpec((tb, KVC), lambda i: (i, (HYC + QC) // KVC)),
            pl.BlockSpec((1, QC), lambda i: (0, 0)),
            pl.BlockSpec((1, KVC), lambda i: (0, 0)),
            pl.BlockSpec((QC, QC), lambda i: (0, 0)),
            pl.BlockSpec((KVC, KVC), lambda i: (0, 0)),
            tab, tab,
        ],
        out_specs=[pl.BlockSpec((tb, QC), lambda i: (i, 0)), pl.BlockSpec((tb, KVC), lambda i: (i, 0))],
        out_shape=[jax.ShapeDtypeStruct((nrows, QC), F32), jax.ShapeDtypeStruct((nrows, KVC), F32)],
        compiler_params=_cp(("arbitrary",)),
        name="qk_norm_rope",
    )(p, p, jnp.tile(qg, NHEADS).reshape(1, QC), jnp.tile(kg, NKV).reshape(1, KVC),
      _block_diag_mean(QC), _block_diag_mean(KVC), cos, sin)


def _attn_kernel(sink_ref, q_ref, kp_ref, kc_ref, kn_ref, vp_ref, vc_ref, vn_ref, kx_ref, vx_ref, o_ref):
    i = pl.program_id(0)
    lat_blocks = T_LAT // TQ
    seq_blocks = SEQ // TQ
    is_lat = i < lat_blocks
    blk = i % seq_blocks
    kwin = jnp.concatenate([kp_ref[...], kc_ref[...], kn_ref[...]], axis=0).astype(BF16)
    vwin = jnp.concatenate([vp_ref[...], vc_ref[...], vn_ref[...]], axis=0).astype(BF16)
    kctx = kx_ref[...].astype(BF16)
    vctx = vx_ref[...].astype(BF16)
    q = q_ref[...].astype(BF16)
    rows = GROUP * TQ
    r = lax.broadcasted_iota(jnp.int32, (rows, 3 * TQ), 0) % TQ
    j = lax.broadcasted_iota(jnp.int32, (rows, 3 * TQ), 1)
    kpos = blk * TQ - TQ + j
    ok = (jnp.abs(j - TQ - r) <= WINDOW) & (kpos >= 0) & (kpos < SEQ) & is_lat
    dn = (((1,), (1,)), ((), ()))
    outs = []
    for h in range(NKV):
        cols = slice(h * HEAD, (h + 1) * HEAD)
        qs = jnp.concatenate([q[:, (GROUP * h + g) * HEAD:(GROUP * h + g + 1) * HEAD] for g in range(GROUP)], axis=0)
        s_w = lax.dot_general(qs, kwin[:, cols], dn, preferred_element_type=F32)
        s_w = jnp.where(ok, s_w, NEG)
        s_c = lax.dot_general(qs, kctx[:, cols], dn, preferred_element_type=F32)
        sk = jnp.concatenate([jnp.full((TQ, 1), sink_ref[GROUP * h + g], F32) for g in range(GROUP)], axis=0)
        m = jnp.maximum(jnp.maximum(jnp.max(s_w, axis=1, keepdims=True), jnp.max(s_c, axis=1, keepdims=True)), sk)
        p_w = jnp.exp(s_w - m)
        p_c = jnp.exp(s_c - m)
        den = jnp.sum(p_w, axis=1, keepdims=True) + jnp.sum(p_c, axis=1, keepdims=True) + jnp.exp(sk - m)
        o = (_dot(p_w.astype(BF16), vwin[:, cols]) + _dot(p_c.astype(BF16), vctx[:, cols])) / den
        outs += [o[g * TQ:(g + 1) * TQ] for g in range(GROUP)]
    o_ref[...] = jnp.concatenate(outs, axis=1)


def _attention(qn, kn, p, nrows, sink):
    nq = nrows // TQ
    lat_blocks = T_LAT // TQ
    seq_blocks = SEQ // TQ
    vcol = (HYC + QC + KVC) // KVC

    def seq_of(i):
        return jnp.where(i < lat_blocks, i // seq_blocks, (i - lat_blocks) // (CTXL // TQ))

    prev = lambda i: jnp.maximum(i - 1, 0)
    nxt = lambda i: jnp.minimum(i + 1, nq - 1)
    ctx_blk = lambda i: T_LAT // CTXL + seq_of(i)
    grid_spec = pltpu.PrefetchScalarGridSpec(
        num_scalar_prefetch=1,
        grid=(nq,),
        in_specs=[
            pl.BlockSpec((TQ, QC), lambda i, s: (i, 0)),
            pl.BlockSpec((TQ, KVC), lambda i, s: (prev(i), 0)),
            pl.BlockSpec((TQ, KVC), lambda i, s: (i, 0)),
            pl.BlockSpec((TQ, KVC), lambda i, s: (nxt(i), 0)),
            pl.BlockSpec((TQ, KVC), lambda i, s: (prev(i), vcol)),
            pl.BlockSpec((TQ, KVC), lambda i, s: (i, vcol)),
            pl.BlockSpec((TQ, KVC), lambda i, s: (nxt(i), vcol)),
            pl.BlockSpec((CTXL, KVC), lambda i, s: (ctx_blk(i), 0)),
            pl.BlockSpec((CTXL, KVC), lambda i, s: (ctx_blk(i), vcol)),
        ],
        out_specs=pl.BlockSpec((TQ, QC), lambda i, s: (i, 0)),
    )
    return pl.pallas_call(
        _attn_kernel,
        grid_spec=grid_spec,
        out_shape=jax.ShapeDtypeStruct((nrows, QC), F32),
        compiler_params=_cp(("arbitrary",)),
        name="windowed_attention",
    )(sink, qn, kn, kn, kn, p, p, p, kn, p)


def _outproj_kernel(*refs, has_ctx):
    if has_ctx:
        x_ref, hl_ref, hc_ref, at_ref, w_ref, g_ref, o_ref = refs
    else:
        x_ref, hl_ref, at_ref, w_ref, g_ref, o_ref = refs
    i = pl.program_id(0)
    grp = _group(i * TB)
    hy = hl_ref[...]
    if has_ctx:
        hy = jnp.where(i < T_LAT // TB, hy, hc_ref[...])
    mix = _dot(hy.astype(BF16), w_ref[:HY, :]) + _dot(at_ref[...].astype(BF16), w_ref[HY:, :])
    o_ref[...] = x_ref[...] + g_ref[pl.ds(grp, 1), :] * mix


def _outproj(x, nrows, hy_l, hy_c, att, w_bf, mods, l):
    has_ctx = hy_c is not None
    lat_blocks = T_LAT // TB
    in_specs = [pl.BlockSpec((TB, D), lambda i: (i, 0)),
                pl.BlockSpec((TB, HY), lambda i: (jnp.minimum(i, lat_blocks - 1), 0))]
    args = [x, hy_l]
    if has_ctx:
        in_specs.append(pl.BlockSpec((TB, HY), lambda i: (jnp.maximum(i - lat_blocks, 0), 0)))
        args.append(hy_c)
    in_specs += [pl.BlockSpec((TB, QC), lambda i: (i, 0)), pl.BlockSpec((HY + QC, D), lambda i: (0, 0)),
                 _mod_spec(l, 2)]
    args += [att, w_bf, mods]
    return pl.pallas_call(
        functools.partial(_outproj_kernel, has_ctx=has_ctx),
        grid=(nrows // TB,),
        in_specs=in_specs,
        out_specs=pl.BlockSpec((TB, D), lambda i: (i, 0)),
        out_shape=jax.ShapeDtypeStruct((nrows, D), F32),
        compiler_params=_cp(("arbitrary",)),
        name="mixer_out_proj",
    )(*args)


HALO = 16


def _conf_kernel(a_ref, ap_ref, an_ref, x_ref, dw_ref, dwb_ref, lg_ref, lb_ref, w_ref, b_ref, g_ref, o_ref):
    tb = TBR
    row0 = pl.program_id(0) * tb
    grp = _group(row0)
    slen = jnp.where(row0 < T_LAT, SEQ, CTXL)
    first = row0 % slen == 0
    last = (row0 + tb) % slen == 0
    ap = jnp.where(first, 0.0, ap_ref[...])
    an = jnp.where(last, 0.0, an_ref[...])
    ext = jnp.concatenate([ap, a_ref[...], an], axis=0)
    pad = (CONF_K - 1) // 2
    acc = jnp.zeros((tb, D), F32)
    next_ = ext.shape[0]
    for r in range(8):
        er = ext if r == 0 else pltpu.roll(ext, next_ - r, axis=0)
        for q in range((HALO + 8) // 8 + 1):
            tap = 8 * q + r - (HALO - pad)
            if 0 <= tap < CONF_K and 8 * q + r + tb <= next_:
                acc = acc + er[8 * q:8 * q + tb] * dw_ref[tap:tap + 1, :]
    a = acc + dwb_ref[...]
    mu = jnp.mean(a, axis=-1, keepdims=True)
    var = jnp.mean(jnp.square(a - mu), axis=-1, keepdims=True)
    y = (a - mu) * lax.rsqrt(var + EPS) * lg_ref[...] + lb_ref[...]
    y = y * jax.nn.sigmoid(y)
    out = _dot(y.astype(BF16), w_ref[...]) + b_ref[...]
    o_ref[...] = x_ref[...] + g_ref[pl.ds(grp, 1), :] * out


def _conformer_tail(a, x, nrows, dw_w, dw_b, ln_g, ln_b, w2_bf, b2, mods, l):
    tb = TBR
    hb = tb // HALO
    nhalo = nrows // HALO
    row = lambda v: v.reshape(1, D)
    full = lambda shape: pl.BlockSpec(shape, lambda i: (0,) * len(shape))
    return pl.pallas_call(
        _conf_kernel,
        grid=(nrows // tb,),
        in_specs=[
            pl.BlockSpec((tb, D), lambda i: (i, 0)),
            pl.BlockSpec((HALO, D), lambda i: (jnp.maximum(i * hb - 1, 0), 0)),
            pl.BlockSpec((HALO, D), lambda i: (jnp.minimum((i + 1) * hb, nhalo - 1), 0)),
            pl.BlockSpec((tb, D), lambda i: (i, 0)),
            full((CONF_K, D)), full((1, D)), full((1, D)), full((1, D)), full((D, D)), full((1, D)),
            _mod_spec(l, 2),
        ],
        out_specs=pl.BlockSpec((tb, D), lambda i: (i, 0)),
        out_shape=jax.ShapeDtypeStruct((nrows, D), F32),
        compiler_params=_cp(("arbitrary",)),
        name="conformer_tail",
    )(a, a, a, x, dw_w, row(dw_b), row(ln_g), row(ln_b), w2_bf, row(b2), mods)


def _router_kernel(x_ref, g_ref, sh_ref, sc_ref, rw_ref, rb_ref, tri_ref, h_ref, route_ref, cnt_ref, carry_ref):
    i = pl.program_id(0)

    @pl.when(i == 0)
    def _():
        carry_ref[...] = jnp.zeros_like(carry_ref)

    grp = _group(i * TBR)
    h = _norm_mod(x_ref[...], g_ref[...], sh_ref[pl.ds(grp, 1), :], sc_ref[pl.ds(grp, 1), :])
    h_ref[...] = h
    lg = _dot3(h, rw_ref[...]) + rb_ref[...]
    lane = lax.broadcasted_iota(jnp.int32, lg.shape, 1)
    vals, idxs, hots = [], [], []
    for _ in range(TOPK):
        m = jnp.max(lg, axis=1, keepdims=True)
        idx = jnp.min(jnp.where(lg == m, lane, LANES), axis=1, keepdims=True)
        hot = lane == idx
        vals.append(m)
        idxs.append(idx)
        hots.append(hot)
        lg = jnp.where(hot, -3e38, lg)
    ex = [jnp.exp(v - vals[0]) for v in vals]
    den = ex[0] + ex[1] + ex[2] + ex[3]
    msum = jnp.zeros(lg.shape, F32)
    for hot in hots:
        msum = msum + jnp.where(hot, 1.0, 0.0)
    base = _dot(tri_ref[...], msum.astype(BF16)) + carry_ref[0:1, :]
    route = jnp.zeros(lg.shape, F32)
    for k in range(TOPK):
        pos = jnp.sum(jnp.where(hots[k], base, 0.0), axis=1, keepdims=True)
        route = route + jnp.where(lane == k, idxs[k].astype(F32), 0.0)
        route = route + jnp.where(lane == TOPK + k, ex[k] / den, 0.0)
        route = route + jnp.where(lane == 2 * TOPK + k, pos, 0.0)
    route_ref[...] = route
    carry_ref[0:1, :] = carry_ref[0:1, :] + jnp.sum(msum, axis=0, keepdims=True)
    cnt_ref[...] = carry_ref[...]


def _router(x, nrows, g, mods, l, rw, rb):
    tb = TBR
    rwp = jnp.pad(rw, ((0, 0), (0, LANES - NE)))
    rbp = jnp.concatenate([rb, jnp.full((LANES - NE,), NEG, F32)]).reshape(1, LANES)
    r = np.arange(tb)
    tri = jnp.asarray(r[None, :] < r[:, None], dtype=F32).astype(BF16)
    return pl.pallas_call(
        _router_kernel,
        grid=(nrows // tb,),
        in_specs=[
            pl.BlockSpec((tb, D), lambda i: (i, 0)),
            pl.BlockSpec((1, D), lambda i: (0, 0)),
            _mod_spec(l, 3), _mod_spec(l, 4),
            pl.BlockSpec((D, LANES), lambda i: (0, 0)),
            pl.BlockSpec((1, LANES), lambda i: (0, 0)),
            pl.BlockSpec((tb, tb), lambda i: (0, 0)),
        ],
        out_specs=[
            pl.BlockSpec((tb, D), lambda i: (i, 0)),
            pl.BlockSpec((tb, LANES), lambda i: (i, 0)),
            pl.BlockSpec((8, LANES), lambda i: (0, 0)),
        ],
        out_shape=[
            jax.ShapeDtypeStruct((nrows, D), F32),
            jax.ShapeDtypeStruct((nrows, LANES), F32),
            jax.ShapeDtypeStruct((8, LANES), F32),
        ],
        scratch_shapes=[pltpu.VMEM((8, LANES), F32)],
        compiler_params=_cp(("arbitrary",)),
        name="moe_router",
    )(x, g.reshape(1, D), mods, mods, rwp, rbp, tri)


def _row_copy(src_ref, s, dst_ref, d, sem):
    return pltpu.make_async_copy(src_ref.at[pl.ds(s, 1)], dst_ref.at[pl.ds(d, 1)], sem)


def _dispatch_kernel(dest_ref, h_ref, rows_in_ref, rows_ref, sem):
    del rows_in_ref
    base = pl.program_id(0) * TBR

    def issue(t, carry):
        for k in range(TOPK):
            _row_copy(h_ref, t, rows_ref, dest_ref[(base + t) * TOPK + k], sem).start(priority=k % 2)
        return carry

    def drain(t, carry):
        for k in range(TOPK):
            _row_copy(h_ref, t, rows_ref, dest_ref[(base + t) * TOPK + k], sem).wait()
        return carry

    lax.fori_loop(0, TBR, issue, 0)
    lax.fori_loop(0, TBR, drain, 0)


def _dispatch(dest, h, nrows, rows0):
    grid_spec = pltpu.PrefetchScalarGridSpec(
        num_scalar_prefetch=1,
        grid=(nrows // TBR,),
        in_specs=[pl.BlockSpec((TBR, D), lambda i, d: (i, 0)), pl.BlockSpec(memory_space=pl.ANY)],
        out_specs=pl.BlockSpec(memory_space=pl.ANY),
        scratch_shapes=[pltpu.SemaphoreType.DMA(())],
    )
    return pl.pallas_call(
        _dispatch_kernel,
        grid_spec=grid_spec,
        out_shape=jax.ShapeDtypeStruct(rows0.shape, F32),
        input_output_aliases={2: 0},
        compiler_params=_cp(("arbitrary",)),
        name="moe_dispatch",
    )(dest, h, rows0)


def _expert_kernel(be_ref, nu_ref, x_ref, wg_ref, bg_ref, wu_ref, bu_ref, wd_ref, bd_ref, o_ref,
                   wgb_ref, wub_ref, wdb_ref):
    j = pl.program_id(0)
    e = be_ref[j]
    prev = be_ref[jnp.maximum(j - 1, 0)]

    @pl.when((j == 0) | (e != prev))
    def _():
        wgb_ref[...] = wg_ref[...].astype(BF16)
        wub_ref[...] = wu_ref[...].astype(BF16)
        wdb_ref[...] = wd_ref[...].astype(BF16)

    @pl.when(j < nu_ref[0])
    def _():
        xb = x_ref[...].astype(BF16)
        g = jnp.minimum(_dot(xb, wgb_ref[...]) + bg_ref[...], SWIGLU_LIMIT)
        u = jnp.clip(_dot(xb, wub_ref[...]) + bu_ref[...], -SWIGLU_LIMIT, SWIGLU_LIMIT)
        a = g * jax.nn.sigmoid(SWIGLU_ALPHA * g) * (u + 1.0)
        o_ref[...] = _dot(a.astype(BF16), wdb_ref[...]) + bd_ref[...]

    @pl.when(j >= nu_ref[0])
    def _():
        o_ref[...] = jnp.zeros_like(o_ref)


def _experts(blk_exp, nused, rows, l, wg, bg, wu, bu, wd, bd):
    nrows = rows.shape[0]
    wspec = pl.BlockSpec((None, None, D, D), lambda j, be, nu: (l, be[j], 0, 0))
    bspec = pl.BlockSpec((None, None, 1, D), lambda j, be, nu: (l, be[j], 0, 0))
    grid_spec = pltpu.PrefetchScalarGridSpec(
        num_scalar_prefetch=2,
        grid=(nrows // BM,),
        in_specs=[pl.BlockSpec((BM, D), lambda j, be, nu: (jnp.minimum(j, nu[0] - 1), 0)),
                  wspec, bspec, wspec, bspec, wspec, bspec],
        out_specs=pl.BlockSpec((BM, D), lambda j, be, nu: (j, 0)),
        scratch_shapes=[pltpu.VMEM((D, D), BF16)] * 3,
    )
    b3 = lambda b: b.reshape(DEPTH, NE, 1, D)
    return pl.pallas_call(
        _expert_kernel,
        grid_spec=grid_spec,
        out_shape=jax.ShapeDtypeStruct((nrows, D), F32),
        compiler_params=_cp(("arbitrary",)),
        name="moe_experts",
    )(blk_exp, nused, rows, wg, b3(bg), wu, b3(bu), wd, b3(bd))


def _combine_kernel(dest_ref, x_ref, route_ref, g_ref, y_ref, o_ref, buf_ref, sem):
    base = pl.program_id(0) * TBC
    grp = _group(base)

    def issue(t, carry):
        for k in range(TOPK):
            _row_copy(y_ref, dest_ref[(base + t) * TOPK + k], buf_ref.at[k], t, sem).start(priority=k % 2)
        return carry

    def drain(t, carry):
        for k in range(TOPK):
            _row_copy(y_ref, dest_ref[(base + t) * TOPK + k], buf_ref.at[k], t, sem).wait()
        return carry

    lax.fori_loop(0, TBC, issue, 0)
    lax.fori_loop(0, TBC, drain, 0)
    acc = jnp.zeros((TBC, D), F32)
    for k in range(TOPK):
        acc = acc + route_ref[:, TOPK + k:TOPK + k + 1] * buf_ref[k]
    o_ref[...] = x_ref[...] + g_ref[pl.ds(grp, 1), :] * acc


def _combine(dest, x, nrows, route, mods, l, y_rows):
    grid_spec = pltpu.PrefetchScalarGridSpec(
        num_scalar_prefetch=1,
        grid=(nrows // TBC,),
        in_specs=[pl.BlockSpec((TBC, D), lambda i, d: (i, 0)),
                  pl.BlockSpec((TBC, LANES), lambda i, d: (i, 0)),
                  _mod_spec(l, 5),
                  pl.BlockSpec(memory_space=pl.ANY)],
        out_specs=pl.BlockSpec((TBC, D), lambda i, d: (i, 0)),
        scratch_shapes=[pltpu.VMEM((TOPK, TBC, D), F32), pltpu.SemaphoreType.DMA(())],
    )
    return pl.pallas_call(
        _combine_kernel,
        grid_spec=grid_spec,
        out_shape=jax.ShapeDtypeStruct((nrows, D), F32),
        compiler_params=_cp(("arbitrary",)),
        name="moe_combine",
    )(dest, x, route, mods, y_rows)


def _moe(x, nrows, g, mods, l, rw, rb, wg, bg, wu, bu, wd, bd):
    h, route, cnt = _router(x, nrows, g, mods, l, rw, rb)
    counts = cnt[0, :NE].astype(jnp.int32)
    pcounts = (counts + BM - 1) // BM * BM
    pends = jnp.cumsum(pcounts)
    pstarts = pends - pcounts
    top_i = route[:, :TOPK].astype(jnp.int32)
    pos = route[:, 2 * TOPK:3 * TOPK].astype(jnp.int32)
    start_of = jnp.sum(jnp.where(top_i[..., None] == jnp.arange(NE), pstarts, 0), axis=-1)
    dest = (start_of + pos).reshape(-1)
    n_rows = nrows * TOPK + NE * BM
    n_blk = n_rows // BM
    nused = pends[-1] // BM
    blk = jnp.minimum(jnp.arange(n_blk), nused - 1) * BM
    blk_exp = jnp.minimum(jnp.sum(blk[:, None] >= pends[None, :], axis=1), NE - 1).astype(jnp.int32)
    rows = _dispatch(dest, h, nrows, jnp.zeros((n_rows, D), F32))
    y_rows = _experts(blk_exp, nused.reshape(1).astype(jnp.int32), rows, l, wg, bg, wu, bu, wd, bd)
    return _combine(dest, x, nrows, route, mods, l, y_rows)


def _hyena(u, hr, hi, skip, tabs, n1, n2):
    nc = HY // 128
    br, bi = _fft_a(u, 0, tabs, n1, n2)
    dr, di = _fft_b(br, bi, hr, hi, 0, tabs, n1, n2)
    z1 = _fft_c(dr, di, u, 0, u, nc, skip[0:1], tabs, n1, n2)
    br, bi = _fft_a(z1, 0, tabs, n1, n2)
    dr, di = _fft_b(br, bi, hr, hi, HY // 256, tabs, n1, n2)
    return _fft_c(dr, di, z1, 0, u, 2 * nc, skip[1:2], tabs, n1, n2)


def _filter_spectra(ls, filt, tabs, n1, n2):
    buf = _filters(ls, *filt)
    fr, fi = _fft_a_filter(buf, tabs, n1, n2)
    return _fft_b_filter(fr, fi, tabs, n1, n2)


FFT_LAT = (128, 64)
FFT_CTX = (32, 16)


def kernel(x, c, ctx, c_ctx, mod_w, mod_b, norm1_g, norm2_g, ev_w_in, ev_w_out, hy_conv_w, hy_conv_b, hy_w1, hy_b1, hy_w2, hy_b2, hy_w3, hy_b3, hy_w_out, hy_freq, hy_skip, q_norm_g, k_norm_g, attn_sink, cf_w1, cf_b1, cf_dw_w, cf_dw_b, cf_ln_g, cf_ln_b, cf_w2, cf_b2, moe_router_w, moe_router_b, moe_w_gate, moe_b_gate, moe_w_up, moe_b_up, moe_w_down, moe_b_down):
    cc8 = jnp.concatenate([c, c_ctx[None, :], jnp.zeros((8 - NB - 1, D), F32)], axis=0)
    mods = _adaln(cc8, mod_w, mod_b)
    xs = jnp.concatenate([x.reshape(T_LAT, D), ctx.reshape(T_CTX, D)], axis=0)
    cos, sin = _rope_tables()
    tabs_l = _fft_tables(*FFT_LAT)
    tabs_c = _fft_tables(*FFT_CTX)
    nrows = T_ALL
    for l in range(DEPTH):
        ctx_live = l < (DEPTH - 1) // 2 * 2
        if l % 2 == 0:
            e = l // 2
            filt = (hy_w1[e], hy_b1[e], hy_w2[e], hy_b2[e], hy_w3[e], hy_b3[e], hy_w_out[e], hy_freq[e])
            p = _nmm(xs, T_ALL, norm1_g[l], mods, l, ev_w_in[e].astype(BF16))
            hr, hi = _filter_spectra(SEQ, filt, tabs_l, *FFT_LAT)
            u = _sconv(p, SEQ, 0, hy_conv_w[e], hy_conv_b[e])
            hy_l = _hyena(u, hr, hi, hy_skip[e], tabs_l, *FFT_LAT)
            hy_c = None
            if ctx_live:
                hrc, hic = _filter_spectra(CTXL, filt, tabs_c, *FFT_CTX)
                uc = _sconv(p, CTXL, T_LAT // CTXL, hy_conv_w[e], hy_conv_b[e])
                hy_c = _hyena(uc, hrc, hic, hy_skip[e], tabs_c, *FFT_CTX)
            qn, kn = _qkprep(p, T_ALL, q_norm_g[e], k_norm_g[e], cos, sin)
            nrows = T_ALL if ctx_live else T_LAT
            att = _attention(qn, kn, p, nrows, attn_sink[e])
            xs = _outproj(xs, nrows, hy_l, hy_c, att, ev_w_out[e].astype(BF16), mods, l)
        else:
            o = l // 2
            a = _nmm(xs, nrows, norm1_g[l], mods, l, cf_w1[o].astype(BF16), cf_b1[o])
            xs = _conformer_tail(a, xs, nrows, cf_dw_w[o], cf_dw_b[o], cf_ln_g[o], cf_ln_b[o],
                                 cf_w2[o].astype(BF16), cf_b2[o], mods, l)
        xs = _moe(xs, nrows, norm2_g[l], mods, l, moe_router_w[l], moe_router_b[l], moe_w_gate, moe_b_gate,
                  moe_w_up, moe_b_up, moe_w_down, moe_b_down)
    return xs[:T_LAT].reshape(NB, SEQ, D)
```

```python
import functools
import math

import numpy as np
import jax
import jax.numpy as jnp
from jax import lax
from jax.experimental import pallas as pl
from jax.experimental.pallas import tpu as pltpu

F32 = jnp.float32
BF16 = jnp.bfloat16

D = 1024
NB = 4
SEQ = 4096
CTXL = 256
DEPTH = 4
GRID_W = 64
T_LAT = NB * SEQ
T_CTX = NB * CTXL
T_ALL = T_LAT + T_CTX

HY = 512
HYC = 3 * HY
QC = 512
KVC = 128
INC = HYC + QC + 2 * KVC
HEAD = 64
NHEADS = 8
NKV = 2
GROUP = 4
WINDOW = 128
ROPE_BASE = 10000.0
HY_BANDS = 16
HY_HID = 64
CONF_K = 31
NE = 32
TOPK = 4
SWIGLU_LIMIT = 7.0
SWIGLU_ALPHA = 1.702
EPS = 1e-6
NEG = -1e30

LANES = 128
VMEM_LIMIT = 56 * 1024 * 1024

TB = 512
TBR = 256
TBC = 128
BM = 256
TQ = 128


def _cp(sem, vmem=VMEM_LIMIT):
    return pltpu.CompilerParams(dimension_semantics=sem, vmem_limit_bytes=vmem)


def _group(row0):
    return jnp.where(row0 < T_LAT, row0 // SEQ, NB)


def _split(a):
    hi = a.astype(BF16)
    lo = (a - hi.astype(F32)).astype(BF16)
    return hi, lo


def _dot(a, b):
    return jnp.dot(a, b, preferred_element_type=F32)


def _dot3(a, b):
    ah, al = _split(a)
    bh, bl = _split(b)
    return _dot(ah, bh) + _dot(ah, bl) + _dot(al, bh)


def _norm_mod(x, g, sh, sc):
    ms = jnp.mean(x * x, axis=-1, keepdims=True)
    h = x * lax.rsqrt(ms + EPS) * g
    return h * (1.0 + sc) + sh


def _mod_spec(l, chunk):
    return pl.BlockSpec((None, 8, D), lambda i, *_: (l, 0, chunk))


def _adaln_kernel(c_ref, w_ref, b_ref, o_ref):
    c = c_ref[...]
    s = (c * jax.nn.sigmoid(c)).astype(BF16)
    o_ref[...] = _dot(s, w_ref[...].astype(BF16)) + b_ref[...]


def _adaln(cc8, mod_w, mod_b):
    tn = 1536
    return pl.pallas_call(
        _adaln_kernel,
        grid=(DEPTH, 6 * D // tn),
        in_specs=[
            pl.BlockSpec((8, D), lambda l, j: (0, 0)),
            pl.BlockSpec((None, D, tn), lambda l, j: (l, 0, j)),
            pl.BlockSpec((None, 1, tn), lambda l, j: (l, 0, j)),
        ],
        out_specs=pl.BlockSpec((None, 8, tn), lambda l, j: (l, 0, j)),
        out_shape=jax.ShapeDtypeStruct((DEPTH, 8, 6 * D), F32),
        compiler_params=_cp(("arbitrary", "arbitrary")),
        name="adaln",
    )(cc8, mod_w, mod_b.reshape(DEPTH, 1, 6 * D))


def _nmm_kernel(x_ref, g_ref, sh_ref, sc_ref, w_ref, *rest, glu):
    grp = _group(pl.program_id(0) * TB)
    h = _norm_mod(x_ref[...], g_ref[...], sh_ref[pl.ds(grp, 1), :], sc_ref[pl.ds(grp, 1), :])
    acc = _dot(h.astype(BF16), w_ref[...])
    if glu:
        b_ref, o_ref = rest
        acc = acc + b_ref[...]
        n = acc.shape[1] // 2
        o_ref[...] = acc[:, :n] * jax.nn.sigmoid(acc[:, n:])
    else:
        (o_ref,) = rest
        o_ref[...] = acc


def _nmm(x, nrows, g, mods, l, w_bf, bias=None):
    n = w_bf.shape[1]
    glu = bias is not None
    in_specs = [
        pl.BlockSpec((TB, D), lambda i: (i, 0)),
        pl.BlockSpec((1, D), lambda i: (0, 0)),
        _mod_spec(l, 0),
        _mod_spec(l, 1),
        pl.BlockSpec((D, n), lambda i: (0, 0)),
    ]
    args = [x, g.reshape(1, D), mods, mods, w_bf]
    if glu:
        in_specs.append(pl.BlockSpec((1, n), lambda i: (0, 0)))
        args.append(bias.reshape(1, n))
    nout = n // 2 if glu else n
    return pl.pallas_call(
        functools.partial(_nmm_kernel, glu=glu),
        grid=(nrows // TB,),
        in_specs=in_specs,
        out_specs=pl.BlockSpec((TB, nout), lambda i: (i, 0)),
        out_shape=jax.ShapeDtypeStruct((nrows, nout), F32),
        compiler_params=_cp(("arbitrary",)),
        name="norm_mod_matmul",
    )(*args)


def _sconv_kernel(u_ref, w_ref, b_ref, o_ref):
    u = u_ref[...]
    n = u.shape[0]
    r = lax.broadcasted_iota(jnp.int32, u.shape, 0)
    up = jnp.where(r == 0, 0.0, pltpu.roll(u, 1, axis=0))
    un = jnp.where(r == n - 1, 0.0, pltpu.roll(u, n - 1, axis=0))
    o_ref[...] = w_ref[0:1, :] * up + w_ref[1:2, :] * u + w_ref[2:3, :] * un + b_ref[...]


def _sconv(p, ls, blk0, w, b):
    cb = 256
    return pl.pallas_call(
        _sconv_kernel,
        grid=(NB, HYC // cb),
        in_specs=[
            pl.BlockSpec((ls, cb), lambda s, j: (blk0 + s, j)),
            pl.BlockSpec((3, cb), lambda s, j: (0, j)),
            pl.BlockSpec((1, cb), lambda s, j: (0, j)),
        ],
        out_specs=pl.BlockSpec((ls, cb), lambda s, j: (s, j)),
        out_shape=jax.ShapeDtypeStruct((NB * ls, HYC), F32),
        compiler_params=_cp(("arbitrary", "arbitrary")),
        name="hyena_short_conv",
    )(p, w, b.reshape(1, HYC))


def _filt_kernel(f_ref, w1_ref, b1_ref, w2_ref, b2_ref, w3_ref, b3_ref, fr_ref, wf_ref, wb_ref, dl_ref,
                 o_ref, hid_ref, *, ls):
    @pl.when(pl.program_id(0) == 0)
    def _():
        fr = fr_ref[...]
        h = jnp.sin(fr * (_dot3(f_ref[...], w1_ref[...]) + b1_ref[...]))
        h = jnp.sin(fr * (_dot3(h, w2_ref[...]) + b2_ref[...]))
        hid_ref[...] = jnp.sin(fr * (_dot3(h, w3_ref[...]) + b3_ref[...]))

    h = hid_ref[...]
    n = lax.broadcasted_iota(jnp.int32, (h.shape[0], 1), 0)
    buf = jnp.concatenate([_dot3(h[:ls], wf_ref[...]), _dot3(h[ls:], wb_ref[...])], axis=0)
    buf = buf * jnp.exp(-f_ref[:, 0:1] * dl_ref[...])
    buf = jnp.where(n == ls, 0.0, buf)
    o_ref[...] = buf * lax.rsqrt(jnp.sum(buf * buf, axis=0, keepdims=True) + EPS)


def _filter_feats(ls):
    n = jnp.arange(2 * ls)
    t = jnp.where(n < ls, n, 2 * ls - n).astype(F32)
    t_norm = t / max(ls - 1, 1)
    bands = jnp.linspace(1e-4, HY_BANDS - 1, HY_BANDS, dtype=F32)
    ang = (2.0 * math.pi / ls) * t[:, None] * bands[None, :]
    feats = jnp.concatenate([t_norm[:, None], jnp.cos(ang), jnp.sin(ang)], axis=-1)
    return jnp.pad(feats, ((0, 0), (0, LANES - feats.shape[1])))


def _filters(ls, w1, b1, w2, b2, w3, b3, w_out, freq):
    n = 2 * ls
    cb = 256
    ncol = 2 * HY
    feats = _filter_feats(ls)
    w1p = jnp.pad(w1, ((0, LANES - w1.shape[0]), (0, 0)))
    log_t = math.log(1e-2)
    deltas = jnp.abs(jnp.linspace(log_t / 1.5, log_t / 0.3, HY, dtype=F32))
    dl = jnp.tile(deltas, 2).reshape(1, ncol)
    full = lambda shape: pl.BlockSpec(shape, lambda j: (0,) * len(shape))
    return pl.pallas_call(
        functools.partial(_filt_kernel, ls=ls),
        grid=(ncol // cb,),
        in_specs=[
            full((n, LANES)), full((LANES, HY_HID)), full((1, HY_HID)), full((HY_HID, HY_HID)), full((1, HY_HID)),
            full((HY_HID, HY_HID)), full((1, HY_HID)), full((1, HY_HID)),
            pl.BlockSpec((HY_HID, cb), lambda j: (0, j)),
            pl.BlockSpec((HY_HID, cb), lambda j: (0, ncol // cb + j)),
            pl.BlockSpec((1, cb), lambda j: (0, j)),
        ],
        out_specs=pl.BlockSpec((n, cb), lambda j: (0, j)),
        out_shape=jax.ShapeDtypeStruct((n, ncol), F32),
        scratch_shapes=[pltpu.VMEM((n, HY_HID), F32)],
        compiler_params=_cp(("arbitrary",)),
        name="hyena_filter_mlp",
    )(feats, w1p, b1.reshape(1, -1), w2, b2.reshape(1, -1), w3, b3.reshape(1, -1), freq.reshape(1, -1),
      w_out, w_out, dl)


def _fft_tables(n1, n2):
    n = n1 * n2
    lh = n1 // 2
    k1 = np.arange(n1)
    a = 2.0 * np.pi * ((k1[:, None] * np.arange(lh)[None, :]) % n1) / n1
    c, s = np.cos(a), np.sin(a)
    m1 = np.block([[c, s], [-s, c]])
    af = 2.0 * np.pi * ((k1[:, None] * k1[None, :]) % n1) / n1
    m1f = np.concatenate([np.cos(af), -np.sin(af)], axis=0)
    k2 = np.arange(n2)
    base = ((k2[:, None] * k2[None, :]) % n2) * n1
    th = 2.0 * np.pi * ((base[None] + k1[:, None, None] * k2[None, None, :]) % n) / n
    ct, st = np.cos(th), np.sin(th)
    m3 = np.concatenate([np.concatenate([ct, st], axis=2), np.concatenate([-st, ct], axis=2)], axis=1)
    ctt, stt = np.swapaxes(ct, 1, 2), np.swapaxes(st, 1, 2)
    m3i = np.concatenate([np.concatenate([ctt, -stt], axis=2), np.concatenate([stt, ctt], axis=2)], axis=1)
    ai = 2.0 * np.pi * ((np.arange(lh)[:, None] * k1[None, :]) % n1) / n1
    ci, si = np.cos(ai) / n, np.sin(ai) / n
    m1i = np.block([[ci, -si], [si, ci]])
    to = lambda x: jnp.asarray(x, dtype=F32).astype(BF16)
    return dict(m1=to(m1), m1f=to(m1f), m3=to(m3), m3i=to(m3i), m1i=to(m1i))


def _ffta_kernel(*refs, n1, n2, real_only):
    if real_only:
        f_ref, m_ref, br_ref, bi_ref = refs
    else:
        z_ref, m_ref, br_ref, bi_ref = refs
    lh = n1 // 2
    ls = lh * n2
    m = m_ref[...]

    def body(j, carry):
        if real_only:
            rhs = f_ref[pl.ds(j, n1, stride=n2), :]
        else:
            rhs = jnp.concatenate([z_ref[pl.ds(j, lh, stride=n2), :], z_ref[pl.ds(ls + j, lh, stride=n2), :]], axis=0)
        a = _dot(m, rhs.astype(BF16))
        br_ref[pl.ds(j, n1, stride=n2), :] = a[:n1]
        bi_ref[pl.ds(j, n1, stride=n2), :] = a[n1:]
        return carry

    lax.fori_loop(0, n2, body, 0, unroll=2)


def _fft_a(z, col0, tabs, n1, n2, cb=128):
    n = n1 * n2
    npair = NB // 2
    out = jax.ShapeDtypeStruct((npair, n, HY), F32)
    return pl.pallas_call(
        functools.partial(_ffta_kernel, n1=n1, n2=n2, real_only=False),
        grid=(npair, HY // cb),
        in_specs=[
            pl.BlockSpec((n, cb), lambda p, c: (p, col0 + c)),
            pl.BlockSpec((2 * n1, n1), lambda p, c: (0, 0)),
        ],
        out_specs=[pl.BlockSpec((None, n, cb), lambda p, c: (p, 0, c))] * 2,
        out_shape=[out, out],
        compiler_params=_cp(("arbitrary", "arbitrary")),
        name="fft_stage_a",
    )(z, tabs["m1"])


def _fft_a_filter(buf, tabs, n1, n2, cb=128):
    n = n1 * n2
    ncol = buf.shape[1]
    out = jax.ShapeDtypeStruct((n, ncol), F32)
    return pl.pallas_call(
        functools.partial(_ffta_kernel, n1=n1, n2=n2, real_only=True),
        grid=(ncol // cb,),
        in_specs=[
            pl.BlockSpec((n, cb), lambda c: (0, c)),
            pl.BlockSpec((2 * n1, n1), lambda c: (0, 0)),
        ],
        out_specs=[pl.BlockSpec((n, cb), lambda c: (0, c))] * 2,
        out_shape=[out, out],
        compiler_params=_cp(("arbitrary",)),
        name="fft_stage_a_filter",
    )(buf, tabs["m1f"])


def _fftb_kernel(*refs, n2, k1b, filter_mode):
    if filter_mode:
        br_ref, bi_ref, m3_ref, or_ref, oi_ref = refs
    else:
        br_ref, bi_ref, hr_ref, hi_ref, m3_ref, m3i_ref, or_ref, oi_ref = refs
    for kk in range(k1b):
        rows = slice(kk * n2, (kk + 1) * n2)
        rhs = jnp.concatenate([br_ref[rows, :], bi_ref[rows, :]], axis=0).astype(BF16)
        x = _dot(m3_ref[kk], rhs)
        xr, xi = x[:n2], x[n2:]
        if filter_mode:
            or_ref[rows, :] = xr
            oi_ref[rows, :] = xi
        else:
            hr, hi = hr_ref[rows, :], hi_ref[rows, :]
            y = jnp.concatenate([xr * hr - xi * hi, xr * hi + xi * hr], axis=0).astype(BF16)
            d = _dot(m3i_ref[kk], y)
            or_ref[rows, :] = d[:n2]
            oi_ref[rows, :] = d[n2:]


def _fft_b(br, bi, hr, hi, hcol0, tabs, n1, n2, cb=256, k1b=8):
    npair, n, _ = br.shape
    rb = k1b * n2
    data = pl.BlockSpec((None, rb, cb), lambda k, p, c: (p, k, c))
    filt = pl.BlockSpec((rb, cb), lambda k, p, c: (k, hcol0 + c))
    tab = pl.BlockSpec((k1b, 2 * n2, 2 * n2), lambda k, p, c: (k, 0, 0))
    out = jax.ShapeDtypeStruct((npair, n, HY), F32)
    return pl.pallas_call(
        functools.partial(_fftb_kernel, n2=n2, k1b=k1b, filter_mode=False),
        grid=(n1 // k1b, npair, HY // cb),
        in_specs=[data, data, filt, filt, tab, tab],
        out_specs=[data, data],
        out_shape=[out, out],
        compiler_params=_cp(("arbitrary",) * 3),
        name="fft_stage_b",
    )(br, bi, hr, hi, tabs["m3"], tabs["m3i"])


def _fft_b_filter(br, bi, tabs, n1, n2, cb=256, k1b=8):
    n, ncol = br.shape
    rb = k1b * n2
    data = pl.BlockSpec((rb, cb), lambda k, c: (k, c))
    tab = pl.BlockSpec((k1b, 2 * n2, 2 * n2), lambda k, c: (k, 0, 0))
    out = jax.ShapeDtypeStruct((n, ncol), F32)
    return pl.pallas_call(
        functools.partial(_fftb_kernel, n2=n2, k1b=k1b, filter_mode=True),
        grid=(n1 // k1b, ncol // cb),
        in_specs=[data, data, tab],
        out_specs=[data, data],
        out_shape=[out, out],
        compiler_params=_cp(("arbitrary",) * 2),
        name="fft_stage_b_filter",
    )(br, bi, tabs["m3"])


def _fftc_kernel(dr_ref, di_ref, m_ref, z_ref, x_ref, sk_ref, o_ref, *, n1, n2):
    lh = n1 // 2
    ls = lh * n2
    m = m_ref[...]
    sk = sk_ref[...]

    def body(j, carry):
        rhs = jnp.concatenate([dr_ref[pl.ds(j, n1, stride=n2), :], di_ref[pl.ds(j, n1, stride=n2), :]], axis=0)
        y = _dot(m, rhs.astype(BF16))
        for half in range(2):
            rows = pl.ds(half * ls + j, lh, stride=n2)
            o_ref[rows, :] = x_ref[rows, :] * (y[half * lh:(half + 1) * lh] + sk * z_ref[rows, :])
        return carry

    lax.fori_loop(0, n2, body, 0, unroll=2)


def _fft_c(dr, di, z, zcol0, gate, gcol0, skip_row, tabs, n1, n2, cb=128):
    npair, n, _ = dr.shape
    spec_d = pl.BlockSpec((None, n, cb), lambda p, c: (p, 0, c))
    return pl.pallas_call(
        functools.partial(_fftc_kernel, n1=n1, n2=n2),
        grid=(npair, HY // cb),
        in_specs=[spec_d, spec_d, pl.BlockSpec((n1, 2 * n1), lambda p, c: (0, 0)),
                  pl.BlockSpec((n, cb), lambda p, c: (p, zcol0 + c)),
                  pl.BlockSpec((n, cb), lambda p, c: (p, gcol0 + c)),
                  pl.BlockSpec((1, cb), lambda p, c: (0, c))],
        out_specs=pl.BlockSpec((n, cb), lambda p, c: (p, c)),
        out_shape=jax.ShapeDtypeStruct((npair * n, HY), F32),
        compiler_params=_cp(("arbitrary", "arbitrary")),
        name="fft_stage_c",
    )(dr, di, tabs["m1i"], z, gate, skip_row)


def _rope_tables():
    rows = SEQ // GRID_W
    row_pos = jnp.repeat(jnp.arange(rows, dtype=F32), GRID_W)
    col_pos = jnp.tile(jnp.arange(GRID_W, dtype=F32), rows)
    nf = HEAD // 4
    inv = ROPE_BASE ** (-jnp.arange(nf, dtype=F32) / nf)
    ar = row_pos[:, None] * inv[None, :]
    ac = col_pos[:, None] * inv[None, :]
    cos = jnp.concatenate([jnp.cos(ar), jnp.cos(ar), jnp.cos(ac), jnp.cos(ac)], axis=-1)
    sin = jnp.concatenate([-jnp.sin(ar), jnp.sin(ar), -jnp.sin(ac), jnp.sin(ac)], axis=-1)
    cos = jnp.tile(cos, (1, LANES // HEAD))
    sin = jnp.tile(sin, (1, LANES // HEAD))
    cos = jnp.concatenate([cos, jnp.ones((TBR, LANES), F32)], axis=0)
    sin = jnp.concatenate([sin, jnp.zeros((TBR, LANES), F32)], axis=0)
    return cos, sin


def _head_norm_rope(x, g, bd, cos, sin, scale):
    w = x.shape[1]
    xx = x * x
    hi, lo = _split(xx)
    ms = _dot(hi, bd) + _dot(lo, bd)
    y = x * lax.rsqrt(ms + EPS) * g
    rep = w // LANES
    if rep > 1:
        cos = jnp.concatenate([cos] * rep, axis=1)
        sin = jnp.concatenate([sin] * rep, axis=1)
    lane = lax.broadcasted_iota(jnp.int32, y.shape, 1)
    half = HEAD // 4
    partner = jnp.where(lane % (2 * half) < half, pltpu.roll(y, w - half, axis=1), pltpu.roll(y, half, axis=1))
    return (y * cos + partner * sin) * scale


def _qkprep_kernel(q_ref, k_ref, qg_ref, kg_ref, bdq_ref, bdk_ref, cos_ref, sin_ref, qo_ref, ko_ref):
    cos, sin = cos_ref[...], sin_ref[...]
    qo_ref[...] = _head_norm_rope(q_ref[...], qg_ref[...], bdq_ref[...], cos, sin, HEAD ** -0.5)
    ko_ref[...] = _head_norm_rope(k_ref[...], kg_ref[...], bdk_ref[...], cos, sin, 1.0)


def _block_diag_mean(w):
    i = np.arange(w)
    return jnp.asarray((i[:, None] // HEAD == i[None, :] // HEAD) / HEAD, dtype=F32).astype(BF16)


def _qkprep(p, nrows, qg, kg, cos, sin):
    tb = TBR
    lat_blocks = T_LAT // tb
    seq_blocks = SEQ // tb
    tab = pl.BlockSpec((tb, LANES), lambda i: (jnp.where(i < lat_blocks, i % seq_blocks, seq_blocks), 0))
    return pl.pallas_call(
        _qkprep_kernel,
        grid=(nrows // tb,),
        in_specs=[
            pl.BlockSpec((tb, QC), lambda i: (i, HYC // QC)),
            pl.BlockSpec((tb, KVC), lambda i: (i, (HYC + QC) // KVC)),
            pl.BlockSpec((1, QC), lambda i: (0, 0)),
            pl.BlockSpec((1, KVC), lambda i: (0, 0)),
            pl.BlockSpec((QC, QC), lambda i: (0, 0)),
            pl.BlockSpec((KVC, KVC), lambda i: (0, 0)),
            tab, tab,
        ],
        out_specs=[pl.BlockSpec((tb, QC), lambda i: (i, 0)), pl.BlockSpec((tb, KVC), lambda i: (i, 0))],
        out_shape=[jax.ShapeDtypeStruct((nrows, QC), F32), jax.ShapeDtypeStruct((nrows, KVC), F32)],
        compiler_params=_cp(("arbitrary",)),
        name="qk_norm_rope",
    )(p, p, jnp.tile(qg, NHEADS).reshape(1, QC), jnp.tile(kg, NKV).reshape(1, KVC),
      _block_diag_mean(QC), _block_diag_mean(KVC), cos, sin)


def _attn_kernel(sink_ref, q_ref, kp_ref, kc_ref, kn_ref, vp_ref, vc_ref, vn_ref, kx_ref, vx_ref, o_ref):
    i = pl.program_id(0)
    lat_blocks = T_LAT // TQ
    seq_blocks = SEQ // TQ
    is_lat = i < lat_blocks
    blk = i % seq_blocks
    kwin = jnp.concatenate([kp_ref[...], kc_ref[...], kn_ref[...]], axis=0).astype(BF16)
    vwin = jnp.concatenate([vp_ref[...], vc_ref[...], vn_ref[...]], axis=0).astype(BF16)
    kctx = kx_ref[...].astype(BF16)
    vctx = vx_ref[...].astype(BF16)
    q = q_ref[...].astype(BF16)
    rows = GROUP * TQ
    r = lax.broadcasted_iota(jnp.int32, (rows, 3 * TQ), 0) % TQ
    j = lax.broadcasted_iota(jnp.int32, (rows, 3 * TQ), 1)
    kpos = blk * TQ - TQ + j
    ok = (jnp.abs(j - TQ - r) <= WINDOW) & (kpos >= 0) & (kpos < SEQ) & is_lat
    dn = (((1,), (1,)), ((), ()))
    outs = []
    for h in range(NKV):
        cols = slice(h * HEAD, (h + 1) * HEAD)
        qs = jnp.concatenate([q[:, (GROUP * h + g) * HEAD:(GROUP * h + g + 1) * HEAD] for g in range(GROUP)], axis=0)
        s_w = lax.dot_general(qs, kwin[:, cols], dn, preferred_element_type=F32)
        s_w = jnp.where(ok, s_w, NEG)
        s_c = lax.dot_general(qs, kctx[:, cols], dn, preferred_element_type=F32)
        sk = jnp.concatenate([jnp.full((TQ, 1), sink_ref[GROUP * h + g], F32) for g in range(GROUP)], axis=0)
        m = jnp.maximum(jnp.maximum(jnp.max(s_w, axis=1, keepdims=True), jnp.max(s_c, axis=1, keepdims=True)), sk)
        p_w = jnp.exp(s_w - m)
        p_c = jnp.exp(s_c - m)
        den = jnp.sum(p_w, axis=1, keepdims=True) + jnp.sum(p_c, axis=1, keepdims=True) + jnp.exp(sk - m)
        o = (_dot(p_w.astype(BF16), vwin[:, cols]) + _dot(p_c.astype(BF16), vctx[:, cols])) / den
        outs += [o[g * TQ:(g + 1) * TQ] for g in range(GROUP)]
    o_ref[...] = jnp.concatenate(outs, axis=1)


def _attention(qn, kn, p, nrows, sink):
    nq = nrows // TQ
    lat_blocks = T_LAT // TQ
    seq_blocks = SEQ // TQ
    vcol = (HYC + QC + KVC) // KVC

    def seq_of(i):
        return jnp.where(i < lat_blocks, i // seq_blocks, (i - lat_blocks) // (CTXL // TQ))

    prev = lambda i: jnp.maximum(i - 1, 0)
    nxt = lambda i: jnp.minimum(i + 1, nq - 1)
    ctx_blk = lambda i: T_LAT // CTXL + seq_of(i)
    grid_spec = pltpu.PrefetchScalarGridSpec(
        num_scalar_prefetch=1,
        grid=(nq,),
        in_specs=[
            pl.BlockSpec((TQ, QC), lambda i, s: (i, 0)),
            pl.BlockSpec((TQ, KVC), lambda i, s: (prev(i), 0)),
            pl.BlockSpec((TQ, KVC), lambda i, s: (i, 0)),
            pl.BlockSpec((TQ, KVC), lambda i, s: (nxt(i), 0)),
            pl.BlockSpec((TQ, KVC), lambda i, s: (prev(i), vcol)),
            pl.BlockSpec((TQ, KVC), lambda i, s: (i, vcol)),
            pl.BlockSpec((TQ, KVC), lambda i, s: (nxt(i), vcol)),
            pl.BlockSpec((CTXL, KVC), lambda i, s: (ctx_blk(i), 0)),
            pl.BlockSpec((CTXL, KVC), lambda i, s: (ctx_blk(i), vcol)),
        ],
        out_specs=pl.BlockSpec((TQ, QC), lambda i, s: (i, 0)),
    )
    return pl.pallas_call(
        _attn_kernel,
        grid_spec=grid_spec,
        out_shape=jax.ShapeDtypeStruct((nrows, QC), F32),
        compiler_params=_cp(("arbitrary",)),
        name="windowed_attention",
    )(sink, qn, kn, kn, kn, p, p, p, kn, p)


def _outproj_kernel(*refs, has_ctx):
    if has_ctx:
        x_ref, hl_ref, hc_ref, at_ref, w_ref, g_ref, o_ref = refs
    else:
        x_ref, hl_ref, at_ref, w_ref, g_ref, o_ref = refs
    i = pl.program_id(0)
    grp = _group(i * TB)
    hy = hl_ref[...]
    if has_ctx:
        hy = jnp.where(i < T_LAT // TB, hy, hc_ref[...])
    mix = _dot(hy.astype(BF16), w_ref[:HY, :]) + _dot(at_ref[...].astype(BF16), w_ref[HY:, :])
    o_ref[...] = x_ref[...] + g_ref[pl.ds(grp, 1), :] * mix


def _outproj(x, nrows, hy_l, hy_c, att, w_bf, mods, l):
    has_ctx = hy_c is not None
    lat_blocks = T_LAT // TB
    in_specs = [pl.BlockSpec((TB, D), lambda i: (i, 0)),
                pl.BlockSpec((TB, HY), lambda i: (jnp.minimum(i, lat_blocks - 1), 0))]
    args = [x, hy_l]
    if has_ctx:
        in_specs.append(pl.BlockSpec((TB, HY), lambda i: (jnp.maximum(i - lat_blocks, 0), 0)))
        args.append(hy_c)
    in_specs += [pl.BlockSpec((TB, QC), lambda i: (i, 0)), pl.BlockSpec((HY + QC, D), lambda i: (0, 0)),
                 _mod_spec(l, 2)]
    args += [att, w_bf, mods]
    return pl.pallas_call(
        functools.partial(_outproj_kernel, has_ctx=has_ctx),
        grid=(nrows // TB,),
        in_specs=in_specs,
        out_specs=pl.BlockSpec((TB, D), lambda i: (i, 0)),
        out_shape=jax.ShapeDtypeStruct((nrows, D), F32),
        compiler_params=_cp(("arbitrary",)),
        name="mixer_out_proj",
    )(*args)


HALO = 16


def _conf_kernel(a_ref, ap_ref, an_ref, x_ref, dw_ref, dwb_ref, lg_ref, lb_ref, w_ref, b_ref, g_ref, o_ref):
    tb = TBR
    row0 = pl.program_id(0) * tb
    grp = _group(row0)
    slen = jnp.where(row0 < T_LAT, SEQ, CTXL)
    first = row0 % slen == 0
    last = (row0 + tb) % slen == 0
    ap = jnp.where(first, 0.0, ap_ref[...])
    an = jnp.where(last, 0.0, an_ref[...])
    ext = jnp.concatenate([ap, a_ref[...], an], axis=0)
    pad = (CONF_K - 1) // 2
    acc = jnp.zeros((tb, D), F32)
    next_ = ext.shape[0]
    for r in range(8):
        er = ext if r == 0 else pltpu.roll(ext, next_ - r, axis=0)
        for q in range((HALO + 8) // 8 + 1):
            tap = 8 * q + r - (HALO - pad)
            if 0 <= tap < CONF_K and 8 * q + r + tb <= next_:
                acc = acc + er[8 * q:8 * q + tb] * dw_ref[tap:tap + 1, :]
    a = acc + dwb_ref[...]
    mu = jnp.mean(a, axis=-1, keepdims=True)
    var = jnp.mean(jnp.square(a - mu), axis=-1, keepdims=True)
    y = (a - mu) * lax.rsqrt(var + EPS) * lg_ref[...] + lb_ref[...]
    y = y * jax.nn.sigmoid(y)
    out = _dot(y.astype(BF16), w_ref[...]) + b_ref[...]
    o_ref[...] = x_ref[...] + g_ref[pl.ds(grp, 1), :] * out


def _conformer_tail(a, x, nrows, dw_w, dw_b, ln_g, ln_b, w2_bf, b2, mods, l):
    tb = TBR
    hb = tb // HALO
    nhalo = nrows // HALO
    row = lambda v: v.reshape(1, D)
    full = lambda shape: pl.BlockSpec(shape, lambda i: (0,) * len(shape))
    return pl.pallas_call(
        _conf_kernel,
        grid=(nrows // tb,),
        in_specs=[
            pl.BlockSpec((tb, D), lambda i: (i, 0)),
            pl.BlockSpec((HALO, D), lambda i: (jnp.maximum(i * hb - 1, 0), 0)),
            pl.BlockSpec((HALO, D), lambda i: (jnp.minimum((i + 1) * hb, nhalo - 1), 0)),
            pl.BlockSpec((tb, D), lambda i: (i, 0)),
            full((CONF_K, D)), full((1, D)), full((1, D)), full((1, D)), full((D, D)), full((1, D)),
            _mod_spec(l, 2),
        ],
        out_specs=pl.BlockSpec((tb, D), lambda i: (i, 0)),
        out_shape=jax.ShapeDtypeStruct((nrows, D), F32),
        compiler_params=_cp(("arbitrary",)),
        name="conformer_tail",
    )(a, a, a, x, dw_w, row(dw_b), row(ln_g), row(ln_b), w2_bf, row(b2), mods)


def _router_kernel(x_ref, g_ref, sh_ref, sc_ref, rw_ref, rb_ref, tri_ref, h_ref, route_ref, cnt_ref, carry_ref):
    i = pl.program_id(0)

    @pl.when(i == 0)
    def _():
        carry_ref[...] = jnp.zeros_like(carry_ref)

    grp = _group(i * TBR)
    h = _norm_mod(x_ref[...], g_ref[...], sh_ref[pl.ds(grp, 1), :], sc_ref[pl.ds(grp, 1), :])
    h_ref[...] = h
    lg = _dot3(h, rw_ref[...]) + rb_ref[...]
    lane = lax.broadcasted_iota(jnp.int32, lg.shape, 1)
    vals, idxs, hots = [], [], []
    for _ in range(TOPK):
        m = jnp.max(lg, axis=1, keepdims=True)
        idx = jnp.min(jnp.where(lg == m, lane, LANES), axis=1, keepdims=True)
        hot = lane == idx
        vals.append(m)
        idxs.append(idx)
        hots.append(hot)
        lg = jnp.where(hot, -3e38, lg)
    ex = [jnp.exp(v - vals[0]) for v in vals]
    den = ex[0] + ex[1] + ex[2] + ex[3]
    msum = jnp.zeros(lg.shape, F32)
    for hot in hots:
        msum = msum + jnp.where(hot, 1.0, 0.0)
    base = _dot(tri_ref[...], msum.astype(BF16)) + carry_ref[0:1, :]
    route = jnp.zeros(lg.shape, F32)
    for k in range(TOPK):
        pos = jnp.sum(jnp.where(hots[k], base, 0.0), axis=1, keepdims=True)
        route = route + jnp.where(lane == k, idxs[k].astype(F32), 0.0)
        route = route + jnp.where(lane == TOPK + k, ex[k] / den, 0.0)
        route = route + jnp.where(lane == 2 * TOPK + k, pos, 0.0)
    route_ref[...] = route
    carry_ref[0:1, :] = carry_ref[0:1, :] + jnp.sum(msum, axis=0, keepdims=True)
    cnt_ref[...] = carry_ref[...]


def _router(x, nrows, g, mods, l, rw, rb):
    tb = TBR
    rwp = jnp.pad(rw, ((0, 0), (0, LANES - NE)))
    rbp = jnp.concatenate([rb, jnp.full((LANES - NE,), NEG, F32)]).reshape(1, LANES)
    r = np.arange(tb)
    tri = jnp.asarray(r[None, :] < r[:, None], dtype=F32).astype(BF16)
    return pl.pallas_call(
        _router_kernel,
        grid=(nrows // tb,),
        in_specs=[
            pl.BlockSpec((tb, D), lambda i: (i, 0)),
            pl.BlockSpec((1, D), lambda i: (0, 0)),
            _mod_spec(l, 3), _mod_spec(l, 4),
            pl.BlockSpec((D, LANES), lambda i: (0, 0)),
            pl.BlockSpec((1, LANES), lambda i: (0, 0)),
            pl.BlockSpec((tb, tb), lambda i: (0, 0)),
        ],
        out_specs=[
            pl.BlockSpec((tb, D), lambda i: (i, 0)),
            pl.BlockSpec((tb, LANES), lambda i: (i, 0)),
            pl.BlockSpec((8, LANES), lambda i: (0, 0)),
        ],
        out_shape=[
            jax.ShapeDtypeStruct((nrows, D), F32),
            jax.ShapeDtypeStruct((nrows, LANES), F32),
            jax.ShapeDtypeStruct((8, LANES), F32),
        ],
        scratch_shapes=[pltpu.VMEM((8, LANES), F32)],
        compiler_params=_cp(("arbitrary",)),
        name="moe_router",
    )(x, g.reshape(1, D), mods, mods, rwp, rbp, tri)


def _row_copy(src_ref, s, dst_ref, d, sem):
    return pltpu.make_async_copy(src_ref.at[pl.ds(s, 1)], dst_ref.at[pl.ds(d, 1)], sem)


def _dispatch_kernel(dest_ref, h_ref, rows_in_ref, rows_ref, sem):
    del rows_in_ref
    base = pl.program_id(0) * TBR

    def issue(t, carry):
        for k in range(TOPK):
            _row_copy(h_ref, t, rows_ref, dest_ref[(base + t) * TOPK + k], sem).start()
        return carry

    lax.fori_loop(0, TBR, issue, 0, unroll=4)
    for k in range(TOPK):
        pltpu.make_async_copy(h_ref, rows_ref.at[pl.ds(0, TBR)], sem).wait()


def _dispatch(dest, h, nrows, rows0):
    grid_spec = pltpu.PrefetchScalarGridSpec(
        num_scalar_prefetch=1,
        grid=(nrows // TBR,),
        in_specs=[pl.BlockSpec((TBR, D), lambda i, d: (i, 0)), pl.BlockSpec(memory_space=pl.ANY)],
        out_specs=pl.BlockSpec(memory_space=pl.ANY),
        scratch_shapes=[pltpu.SemaphoreType.DMA(())],
    )
    return pl.pallas_call(
        _dispatch_kernel,
        grid_spec=grid_spec,
        out_shape=jax.ShapeDtypeStruct(rows0.shape, F32),
        input_output_aliases={2: 0},
        compiler_params=_cp(("arbitrary",)),
        name="moe_dispatch",
    )(dest, h, rows0)


def _expert_kernel(be_ref, nu_ref, x_ref, wg_ref, bg_ref, wu_ref, bu_ref, wd_ref, bd_ref, o_ref,
                   wgb_ref, wub_ref, wdb_ref):
    j = pl.program_id(0)
    e = be_ref[j]
    prev = be_ref[jnp.maximum(j - 1, 0)]

    @pl.when((j == 0) | (e != prev))
    def _():
        wgb_ref[...] = wg_ref[...].astype(BF16)
        wub_ref[...] = wu_ref[...].astype(BF16)
        wdb_ref[...] = wd_ref[...].astype(BF16)

    @pl.when(j < nu_ref[0])
    def _():
        xb = x_ref[...].astype(BF16)
        g = jnp.minimum(_dot(xb, wgb_ref[...]) + bg_ref[...], SWIGLU_LIMIT)
        u = jnp.clip(_dot(xb, wub_ref[...]) + bu_ref[...], -SWIGLU_LIMIT, SWIGLU_LIMIT)
        a = g * jax.nn.sigmoid(SWIGLU_ALPHA * g) * (u + 1.0)
        o_ref[...] = _dot(a.astype(BF16), wdb_ref[...]) + bd_ref[...]

    @pl.when(j >= nu_ref[0])
    def _():
        o_ref[...] = jnp.zeros_like(o_ref)


def _experts(blk_exp, nused, rows, l, wg, bg, wu, bu, wd, bd):
    nrows = rows.shape[0]
    wspec = pl.BlockSpec((None, None, D, D), lambda j, be, nu: (l, be[j], 0, 0))
    bspec = pl.BlockSpec((None, None, 1, D), lambda j, be, nu: (l, be[j], 0, 0))
    grid_spec = pltpu.PrefetchScalarGridSpec(
        num_scalar_prefetch=2,
        grid=(nrows // BM,),
        in_specs=[pl.BlockSpec((BM, D), lambda j, be, nu: (jnp.minimum(j, nu[0] - 1), 0)),
                  wspec, bspec, wspec, bspec, wspec, bspec],
        out_specs=pl.BlockSpec((BM, D), lambda j, be, nu: (j, 0)),
        scratch_shapes=[pltpu.VMEM((D, D), BF16)] * 3,
    )
    b3 = lambda b: b.reshape(DEPTH, NE, 1, D)
    return pl.pallas_call(
        _expert_kernel,
        grid_spec=grid_spec,
        out_shape=jax.ShapeDtypeStruct((nrows, D), F32),
        compiler_params=_cp(("arbitrary",)),
        name="moe_experts",
    )(blk_exp, nused, rows, wg, b3(bg), wu, b3(bu), wd, b3(bd))


def _combine_kernel(dest_ref, x_ref, route_ref, g_ref, y_ref, o_ref, buf_ref, sem):
    i = pl.program_id(0)
    slot = i % 2
    grp = _group(i * TBC)

    def gather(blk, s):
        def issue(t, carry):
            for k in range(TOPK):
                _row_copy(y_ref, dest_ref[(blk * TBC + t) * TOPK + k], buf_ref.at[s, k], t, sem.at[s]).start()
            return carry
        lax.fori_loop(0, TBC, issue, 0, unroll=4)

    @pl.when(i == 0)
    def _():
        gather(0, 0)

    @pl.when(i + 1 < pl.num_programs(0))
    def _():
        gather(i + 1, 1 - slot)

    for k in range(TOPK):
        pltpu.make_async_copy(y_ref.at[pl.ds(0, TBC)], buf_ref.at[slot, k], sem.at[slot]).wait()
    acc = jnp.zeros((TBC, D), F32)
    for k in range(TOPK):
        acc = acc + route_ref[:, TOPK + k:TOPK + k + 1] * buf_ref[slot, k]
    o_ref[...] = x_ref[...] + g_ref[pl.ds(grp, 1), :] * acc


def _combine(dest, x, nrows, route, mods, l, y_rows):
    grid_spec = pltpu.PrefetchScalarGridSpec(
        num_scalar_prefetch=1,
        grid=(nrows // TBC,),
        in_specs=[pl.BlockSpec((TBC, D), lambda i, d: (i, 0)),
                  pl.BlockSpec((TBC, LANES), lambda i, d: (i, 0)),
                  _mod_spec(l, 5),
                  pl.BlockSpec(memory_space=pl.ANY)],
        out_specs=pl.BlockSpec((TBC, D), lambda i, d: (i, 0)),
        scratch_shapes=[pltpu.VMEM((2, TOPK, TBC, D), F32), pltpu.SemaphoreType.DMA((2,))],
    )
    return pl.pallas_call(
        _combine_kernel,
        grid_spec=grid_spec,
        out_shape=jax.ShapeDtypeStruct((nrows, D), F32),
        compiler_params=_cp(("arbitrary",)),
        name="moe_combine",
    )(dest, x, route, mods, y_rows)


def _moe(x, nrows, g, mods, l, rw, rb, wg, bg, wu, bu, wd, bd):
    h, route, cnt = _router(x, nrows, g, mods, l, rw, rb)
    counts = cnt[0, :NE].astype(jnp.int32)
    pcounts = (counts + BM - 1) // BM * BM
    pends = jnp.cumsum(pcounts)
    pstarts = pends - pcounts
    top_i = route[:, :TOPK].astype(jnp.int32)
    pos = route[:, 2 * TOPK:3 * TOPK].astype(jnp.int32)
    start_of = jnp.sum(jnp.where(top_i[..., None] == jnp.arange(NE), pstarts, 0), axis=-1)
    dest = (start_of + pos).reshape(-1)
    n_rows = nrows * TOPK + NE * BM
    n_blk = n_rows // BM
    nused = pends[-1] // BM
    blk = jnp.minimum(jnp.arange(n_blk), nused - 1) * BM
    blk_exp = jnp.minimum(jnp.sum(blk[:, None] >= pends[None, :], axis=1), NE - 1).astype(jnp.int32)
    rows = _dispatch(dest, h, nrows, jnp.zeros((n_rows, D), F32))
    y_rows = _experts(blk_exp, nused.reshape(1).astype(jnp.int32), rows, l, wg, bg, wu, bu, wd, bd)
    return _combine(dest, x, nrows, route, mods, l, y_rows)


def _hyena(u, hr, hi, skip, tabs, n1, n2):
    nc = HY // 128
    br, bi = _fft_a(u, 0, tabs, n1, n2)
    dr, di = _fft_b(br, bi, hr, hi, 0, tabs, n1, n2)
    z1 = _fft_c(dr, di, u, 0, u, nc, skip[0:1], tabs, n1, n2)
    br, bi = _fft_a(z1, 0, tabs, n1, n2)
    dr, di = _fft_b(br, bi, hr, hi, HY // 256, tabs, n1, n2)
    return _fft_c(dr, di, z1, 0, u, 2 * nc, skip[1:2], tabs, n1, n2)


def _filter_spectra(ls, filt, tabs, n1, n2):
    buf = _filters(ls, *filt)
    fr, fi = _fft_a_filter(buf, tabs, n1, n2)
    return _fft_b_filter(fr, fi, tabs, n1, n2)


FFT_LAT = (128, 64)
FFT_CTX = (32, 16)


def kernel(x, c, ctx, c_ctx, mod_w, mod_b, norm1_g, norm2_g, ev_w_in, ev_w_out, hy_conv_w, hy_conv_b, hy_w1, hy_b1, hy_w2, hy_b2, hy_w3, hy_b3, hy_w_out, hy_freq, hy_skip, q_norm_g, k_norm_g, attn_sink, cf_w1, cf_b1, cf_dw_w, cf_dw_b, cf_ln_g, cf_ln_b, cf_w2, cf_b2, moe_router_w, moe_router_b, moe_w_gate, moe_b_gate, moe_w_up, moe_b_up, moe_w_down, moe_b_down):
    cc8 = jnp.concatenate([c, c_ctx[None, :], jnp.zeros((8 - NB - 1, D), F32)], axis=0)
    mods = _adaln(cc8, mod_w, mod_b)
    xs = jnp.concatenate([x.reshape(T_LAT, D), ctx.reshape(T_CTX, D)], axis=0)
    cos, sin = _rope_tables()
    tabs_l = _fft_tables(*FFT_LAT)
    tabs_c = _fft_tables(*FFT_CTX)
    nrows = T_ALL
    for l in range(DEPTH):
        ctx_live = l < (DEPTH - 1) // 2 * 2
        if l % 2 == 0:
            e = l // 2
            filt = (hy_w1[e], hy_b1[e], hy_w2[e], hy_b2[e], hy_w3[e], hy_b3[e], hy_w_out[e], hy_freq[e])
            p = _nmm(xs, T_ALL, norm1_g[l], mods, l, ev_w_in[e].astype(BF16))
            hr, hi = _filter_spectra(SEQ, filt, tabs_l, *FFT_LAT)
            u = _sconv(p, SEQ, 0, hy_conv_w[e], hy_conv_b[e])
            hy_l = _hyena(u, hr, hi, hy_skip[e], tabs_l, *FFT_LAT)
            hy_c = None
            if ctx_live:
                hrc, hic = _filter_spectra(CTXL, filt, tabs_c, *FFT_CTX)
                uc = _sconv(p, CTXL, T_LAT // CTXL, hy_conv_w[e], hy_conv_b[e])
                hy_c = _hyena(uc, hrc, hic, hy_skip[e], tabs_c, *FFT_CTX)
            qn, kn = _qkprep(p, T_ALL, q_norm_g[e], k_norm_g[e], cos, sin)
            nrows = T_ALL if ctx_live else T_LAT
            att = _attention(qn, kn, p, nrows, attn_sink[e])
            xs = _outproj(xs, nrows, hy_l, hy_c, att, ev_w_out[e].astype(BF16), mods, l)
        else:
            o = l // 2
            a = _nmm(xs, nrows, norm1_g[l], mods, l, cf_w1[o].astype(BF16), cf_b1[o])
            xs = _conformer_tail(a, xs, nrows, cf_dw_w[o], cf_dw_b[o], cf_ln_g[o], cf_ln_b[o],
                                 cf_w2[o].astype(BF16), cf_b2[o], mods, l)
        xs = _moe(xs, nrows, norm2_g[l], mods, l, moe_router_w[l], moe_router_b[l], moe_w_gate, moe_b_gate,
                  moe_w_up, moe_b_up, moe_w_down, moe_b_down)
    return xs[:T_LAT].reshape(NB, SEQ, D)
```

```python
import functools
import math

import numpy as np
import jax
import jax.numpy as jnp
from jax import lax
from jax.experimental import pallas as pl
from jax.experimental.pallas import tpu as pltpu

F32 = jnp.float32
BF16 = jnp.bfloat16

D = 1024
NB = 4
SEQ = 4096
CTXL = 256
DEPTH = 4
GRID_W = 64
T_LAT = NB * SEQ
T_CTX = NB * CTXL
T_ALL = T_LAT + T_CTX

HY = 512
HYC = 3 * HY
QC = 512
KVC = 128
INC = HYC + QC + 2 * KVC
HEAD = 64
NHEADS = 8
NKV = 2
GROUP = 4
WINDOW = 128
ROPE_BASE = 10000.0
HY_BANDS = 16
HY_HID = 64
CONF_K = 31
NE = 32
TOPK = 4
SWIGLU_LIMIT = 7.0
SWIGLU_ALPHA = 1.702
EPS = 1e-6
NEG = -1e30

LANES = 128
VMEM_LIMIT = 56 * 1024 * 1024

TB = 512
TBR = 256
TBC = 128
BM = 512
TQ = 128


def _cp(sem, vmem=VMEM_LIMIT):
    return pltpu.CompilerParams(dimension_semantics=sem, vmem_limit_bytes=vmem)


def _group(row0):
    return jnp.where(row0 < T_LAT, row0 // SEQ, NB)


def _split(a):
    hi = a.astype(BF16)
    lo = (a - hi.astype(F32)).astype(BF16)
    return hi, lo


def _dot(a, b):
    return jnp.dot(a, b, preferred_element_type=F32)


def _dot3(a, b):
    ah, al = _split(a)
    bh, bl = _split(b)
    return _dot(ah, bh) + _dot(ah, bl) + _dot(al, bh)


def _norm_mod(x, g, sh, sc):
    ms = jnp.mean(x * x, axis=-1, keepdims=True)
    h = x * lax.rsqrt(ms + EPS) * g
    return h * (1.0 + sc) + sh


def _mod_spec(l, chunk):
    return pl.BlockSpec((None, 8, D), lambda i, *_: (l, 0, chunk))


def _adaln_kernel(c_ref, w_ref, b_ref, o_ref):
    c = c_ref[...]
    s = (c * jax.nn.sigmoid(c)).astype(BF16)
    o_ref[...] = _dot(s, w_ref[...].astype(BF16)) + b_ref[...]


def _adaln(cc8, mod_w, mod_b):
    tn = 1536
    return pl.pallas_call(
        _adaln_kernel,
        grid=(DEPTH, 6 * D // tn),
        in_specs=[
            pl.BlockSpec((8, D), lambda l, j: (0, 0)),
            pl.BlockSpec((None, D, tn), lambda l, j: (l, 0, j)),
            pl.BlockSpec((None, 1, tn), lambda l, j: (l, 0, j)),
        ],
        out_specs=pl.BlockSpec((None, 8, tn), lambda l, j: (l, 0, j)),
        out_shape=jax.ShapeDtypeStruct((DEPTH, 8, 6 * D), F32),
        compiler_params=_cp(("arbitrary", "arbitrary")),
        name="adaln",
    )(cc8, mod_w, mod_b.reshape(DEPTH, 1, 6 * D))


def _nmm_kernel(x_ref, g_ref, sh_ref, sc_ref, w_ref, *rest, glu):
    grp = _group(pl.program_id(0) * TB)
    h = _norm_mod(x_ref[...], g_ref[...], sh_ref[pl.ds(grp, 1), :], sc_ref[pl.ds(grp, 1), :])
    acc = _dot(h.astype(BF16), w_ref[...])
    if glu:
        b_ref, o_ref = rest
        acc = acc + b_ref[...]
        n = acc.shape[1] // 2
        o_ref[...] = acc[:, :n] * jax.nn.sigmoid(acc[:, n:])
    else:
        (o_ref,) = rest
        o_ref[...] = acc


def _nmm(x, nrows, g, mods, l, w_bf, bias=None):
    n = w_bf.shape[1]
    glu = bias is not None
    in_specs = [
        pl.BlockSpec((TB, D), lambda i: (i, 0)),
        pl.BlockSpec((1, D), lambda i: (0, 0)),
        _mod_spec(l, 0),
        _mod_spec(l, 1),
        pl.BlockSpec((D, n), lambda i: (0, 0)),
    ]
    args = [x, g.reshape(1, D), mods, mods, w_bf]
    if glu:
        in_specs.append(pl.BlockSpec((1, n), lambda i: (0, 0)))
        args.append(bias.reshape(1, n))
    nout = n // 2 if glu else n
    return pl.pallas_call(
        functools.partial(_nmm_kernel, glu=glu),
        grid=(nrows // TB,),
        in_specs=in_specs,
        out_specs=pl.BlockSpec((TB, nout), lambda i: (i, 0)),
        out_shape=jax.ShapeDtypeStruct((nrows, nout), F32),
        compiler_params=_cp(("arbitrary",)),
        name="norm_mod_matmul",
    )(*args)


def _sconv_kernel(u_ref, w_ref, b_ref, o_ref):
    u = u_ref[...]
    n = u.shape[0]
    r = lax.broadcasted_iota(jnp.int32, u.shape, 0)
    up = jnp.where(r == 0, 0.0, pltpu.roll(u, 1, axis=0))
    un = jnp.where(r == n - 1, 0.0, pltpu.roll(u, n - 1, axis=0))
    o_ref[...] = w_ref[0:1, :] * up + w_ref[1:2, :] * u + w_ref[2:3, :] * un + b_ref[...]


def _sconv(p, ls, blk0, w, b):
    cb = 256
    return pl.pallas_call(
        _sconv_kernel,
        grid=(NB, HYC // cb),
        in_specs=[
            pl.BlockSpec((ls, cb), lambda s, j: (blk0 + s, j)),
            pl.BlockSpec((3, cb), lambda s, j: (0, j)),
            pl.BlockSpec((1, cb), lambda s, j: (0, j)),
        ],
        out_specs=pl.BlockSpec((ls, cb), lambda s, j: (s, j)),
        out_shape=jax.ShapeDtypeStruct((NB * ls, HYC), F32),
        compiler_params=_cp(("arbitrary", "arbitrary")),
        name="hyena_short_conv",
    )(p, w, b.reshape(1, HYC))


def _filt_kernel(f_ref, w1_ref, b1_ref, w2_ref, b2_ref, w3_ref, b3_ref, fr_ref, wf_ref, wb_ref, dl_ref,
                 o_ref, hid_ref, *, ls):
    @pl.when(pl.program_id(0) == 0)
    def _():
        fr = fr_ref[...]
        h = jnp.sin(fr * (_dot3(f_ref[...], w1_ref[...]) + b1_ref[...]))
        h = jnp.sin(fr * (_dot3(h, w2_ref[...]) + b2_ref[...]))
        hid_ref[...] = jnp.sin(fr * (_dot3(h, w3_ref[...]) + b3_ref[...]))

    h = hid_ref[...]
    n = lax.broadcasted_iota(jnp.int32, (h.shape[0], 1), 0)
    buf = jnp.concatenate([_dot3(h[:ls], wf_ref[...]), _dot3(h[ls:], wb_ref[...])], axis=0)
    buf = buf * jnp.exp(-f_ref[:, 0:1] * dl_ref[...])
    buf = jnp.where(n == ls, 0.0, buf)
    o_ref[...] = buf * lax.rsqrt(jnp.sum(buf * buf, axis=0, keepdims=True) + EPS)


def _filter_feats(ls):
    n = jnp.arange(2 * ls)
    t = jnp.where(n < ls, n, 2 * ls - n).astype(F32)
    t_norm = t / max(ls - 1, 1)
    bands = jnp.linspace(1e-4, HY_BANDS - 1, HY_BANDS, dtype=F32)
    ang = (2.0 * math.pi / ls) * t[:, None] * bands[None, :]
    feats = jnp.concatenate([t_norm[:, None], jnp.cos(ang), jnp.sin(ang)], axis=-1)
    return jnp.pad(feats, ((0, 0), (0, LANES - feats.shape[1])))


def _filters(ls, w1, b1, w2, b2, w3, b3, w_out, freq):
    n = 2 * ls
    cb = 256
    ncol = 2 * HY
    feats = _filter_feats(ls)
    w1p = jnp.pad(w1, ((0, LANES - w1.shape[0]), (0, 0)))
    log_t = math.log(1e-2)
    deltas = jnp.abs(jnp.linspace(log_t / 1.5, log_t / 0.3, HY, dtype=F32))
    dl = jnp.tile(deltas, 2).reshape(1, ncol)
    full = lambda shape: pl.BlockSpec(shape, lambda j: (0,) * len(shape))
    return pl.pallas_call(
        functools.partial(_filt_kernel, ls=ls),
        grid=(ncol // cb,),
        in_specs=[
            full((n, LANES)), full((LANES, HY_HID)), full((1, HY_HID)), full((HY_HID, HY_HID)), full((1, HY_HID)),
            full((HY_HID, HY_HID)), full((1, HY_HID)), full((1, HY_HID)),
            pl.BlockSpec((HY_HID, cb), lambda j: (0, j)),
            pl.BlockSpec((HY_HID, cb), lambda j: (0, ncol // cb + j)),
            pl.BlockSpec((1, cb), lambda j: (0, j)),
        ],
        out_specs=pl.BlockSpec((n, cb), lambda j: (0, j)),
        out_shape=jax.ShapeDtypeStruct((n, ncol), F32),
        scratch_shapes=[pltpu.VMEM((n, HY_HID), F32)],
        compiler_params=_cp(("arbitrary",)),
        name="hyena_filter_mlp",
    )(feats, w1p, b1.reshape(1, -1), w2, b2.reshape(1, -1), w3, b3.reshape(1, -1), freq.reshape(1, -1),
      w_out, w_out, dl)


def _fft_tables(n1, n2):
    n = n1 * n2
    lh = n1 // 2
    k1 = np.arange(n1)
    a = 2.0 * np.pi * ((k1[:, None] * np.arange(lh)[None, :]) % n1) / n1
    c, s = np.cos(a), np.sin(a)
    m1 = np.block([[c, s], [-s, c]])
    af = 2.0 * np.pi * ((k1[:, None] * k1[None, :]) % n1) / n1
    m1f = np.concatenate([np.cos(af), -np.sin(af)], axis=0)
    k2 = np.arange(n2)
    base = ((k2[:, None] * k2[None, :]) % n2) * n1
    th = 2.0 * np.pi * ((base[None] + k1[:, None, None] * k2[None, None, :]) % n) / n
    ct, st = np.cos(th), np.sin(th)
    m3 = np.concatenate([np.concatenate([ct, st], axis=2), np.concatenate([-st, ct], axis=2)], axis=1)
    ctt, stt = np.swapaxes(ct, 1, 2), np.swapaxes(st, 1, 2)
    m3i = np.concatenate([np.concatenate([ctt, -stt], axis=2), np.concatenate([stt, ctt], axis=2)], axis=1)
    ai = 2.0 * np.pi * ((np.arange(lh)[:, None] * k1[None, :]) % n1) / n1
    ci, si = np.cos(ai) / n, np.sin(ai) / n
    m1i = np.block([[ci, -si], [si, ci]])
    to = lambda x: jnp.asarray(x, dtype=F32).astype(BF16)
    return dict(m1=to(m1), m1f=to(m1f), m3=to(m3), m3i=to(m3i), m1i=to(m1i))


def _ffta_kernel(*refs, n1, n2, real_only):
    if real_only:
        f_ref, m_ref, br_ref, bi_ref = refs
    else:
        z_ref, m_ref, br_ref, bi_ref = refs
    lh = n1 // 2
    ls = lh * n2
    m = m_ref[...]

    def body(j, carry):
        if real_only:
            rhs = f_ref[pl.ds(j, n1, stride=n2), :]
        else:
            rhs = jnp.concatenate([z_ref[pl.ds(j, lh, stride=n2), :], z_ref[pl.ds(ls + j, lh, stride=n2), :]], axis=0)
        a = _dot(m, rhs.astype(BF16))
        br_ref[pl.ds(j, n1, stride=n2), :] = a[:n1]
        bi_ref[pl.ds(j, n1, stride=n2), :] = a[n1:]
        return carry

    lax.fori_loop(0, n2, body, 0, unroll=2)


def _fft_a(z, col0, tabs, n1, n2, cb=128):
    n = n1 * n2
    npair = NB // 2
    out = jax.ShapeDtypeStruct((npair, n, HY), F32)
    return pl.pallas_call(
        functools.partial(_ffta_kernel, n1=n1, n2=n2, real_only=False),
        grid=(npair, HY // cb),
        in_specs=[
            pl.BlockSpec((n, cb), lambda p, c: (p, col0 + c)),
            pl.BlockSpec((2 * n1, n1), lambda p, c: (0, 0)),
        ],
        out_specs=[pl.BlockSpec((None, n, cb), lambda p, c: (p, 0, c))] * 2,
        out_shape=[out, out],
        compiler_params=_cp(("arbitrary", "arbitrary")),
        name="fft_stage_a",
    )(z, tabs["m1"])


def _fft_a_filter(buf, tabs, n1, n2, cb=128):
    n = n1 * n2
    ncol = buf.shape[1]
    out = jax.ShapeDtypeStruct((n, ncol), F32)
    return pl.pallas_call(
        functools.partial(_ffta_kernel, n1=n1, n2=n2, real_only=True),
        grid=(ncol // cb,),
        in_specs=[
            pl.BlockSpec((n, cb), lambda c: (0, c)),
            pl.BlockSpec((2 * n1, n1), lambda c: (0, 0)),
        ],
        out_specs=[pl.BlockSpec((n, cb), lambda c: (0, c))] * 2,
        out_shape=[out, out],
        compiler_params=_cp(("arbitrary",)),
        name="fft_stage_a_filter",
    )(buf, tabs["m1f"])


def _fftb_kernel(*refs, n2, k1b, filter_mode):
    if filter_mode:
        br_ref, bi_ref, m3_ref, or_ref, oi_ref = refs
    else:
        br_ref, bi_ref, hr_ref, hi_ref, m3_ref, m3i_ref, or_ref, oi_ref = refs
    for kk in range(k1b):
        rows = slice(kk * n2, (kk + 1) * n2)
        rhs = jnp.concatenate([br_ref[rows, :], bi_ref[rows, :]], axis=0).astype(BF16)
        x = _dot(m3_ref[kk], rhs)
        xr, xi = x[:n2], x[n2:]
        if filter_mode:
            or_ref[rows, :] = xr
            oi_ref[rows, :] = xi
        else:
            hr, hi = hr_ref[rows, :], hi_ref[rows, :]
            y = jnp.concatenate([xr * hr - xi * hi, xr * hi + xi * hr], axis=0).astype(BF16)
            d = _dot(m3i_ref[kk], y)
            or_ref[rows, :] = d[:n2]
            oi_ref[rows, :] = d[n2:]


def _fft_b(br, bi, hr, hi, hcol0, tabs, n1, n2, cb=256, k1b=8):
    npair, n, _ = br.shape
    rb = k1b * n2
    data = pl.BlockSpec((None, rb, cb), lambda k, p, c: (p, k, c))
    filt = pl.BlockSpec((rb, cb), lambda k, p, c: (k, hcol0 + c))
    tab = pl.BlockSpec((k1b, 2 * n2, 2 * n2), lambda k, p, c: (k, 0, 0))
    out = jax.ShapeDtypeStruct((npair, n, HY), F32)
    return pl.pallas_call(
        functools.partial(_fftb_kernel, n2=n2, k1b=k1b, filter_mode=False),
        grid=(n1 // k1b, npair, HY // cb),
        in_specs=[data, data, filt, filt, tab, tab],
        out_specs=[data, data],
        out_shape=[out, out],
        compiler_params=_cp(("arbitrary",) * 3),
        name="fft_stage_b",
    )(br, bi, hr, hi, tabs["m3"], tabs["m3i"])


def _fft_b_filter(br, bi, tabs, n1, n2, cb=256, k1b=8):
    n, ncol = br.shape
    rb = k1b * n2
    data = pl.BlockSpec((rb, cb), lambda k, c: (k, c))
    tab = pl.BlockSpec((k1b, 2 * n2, 2 * n2), lambda k, c: (k, 0, 0))
    out = jax.ShapeDtypeStruct((n, ncol), F32)
    return pl.pallas_call(
        functools.partial(_fftb_kernel, n2=n2, k1b=k1b, filter_mode=True),
        grid=(n1 // k1b, ncol // cb),
        in_specs=[data, data, tab],
        out_specs=[data, data],
        out_shape=[out, out],
        compiler_params=_cp(("arbitrary",) * 2),
        name="fft_stage_b_filter",
    )(br, bi, tabs["m3"])


def _fftc_kernel(dr_ref, di_ref, m_ref, z_ref, x_ref, sk_ref, o_ref, *, n1, n2):
    lh = n1 // 2
    ls = lh * n2
    m = m_ref[...]
    sk = sk_ref[...]

    def body(j, carry):
        rhs = jnp.concatenate([dr_ref[pl.ds(j, n1, stride=n2), :], di_ref[pl.ds(j, n1, stride=n2), :]], axis=0)
        y = _dot(m, rhs.astype(BF16))
        for half in range(2):
            rows = pl.ds(half * ls + j, lh, stride=n2)
            o_ref[rows, :] = x_ref[rows, :] * (y[half * lh:(half + 1) * lh] + sk * z_ref[rows, :])
        return carry

    lax.fori_loop(0, n2, body, 0, unroll=2)


def _fft_c(dr, di, z, zcol0, gate, gcol0, skip_row, tabs, n1, n2, cb=128):
    npair, n, _ = dr.shape
    spec_d = pl.BlockSpec((None, n, cb), lambda p, c: (p, 0, c))
    return pl.pallas_call(
        functools.partial(_fftc_kernel, n1=n1, n2=n2),
        grid=(npair, HY // cb),
        in_specs=[spec_d, spec_d, pl.BlockSpec((n1, 2 * n1), lambda p, c: (0, 0)),
                  pl.BlockSpec((n, cb), lambda p, c: (p, zcol0 + c)),
                  pl.BlockSpec((n, cb), lambda p, c: (p, gcol0 + c)),
                  pl.BlockSpec((1, cb), lambda p, c: (0, c))],
        out_specs=pl.BlockSpec((n, cb), lambda p, c: (p, c)),
        out_shape=jax.ShapeDtypeStruct((npair * n, HY), F32),
        compiler_params=_cp(("arbitrary", "arbitrary")),
        name="fft_stage_c",
    )(dr, di, tabs["m1i"], z, gate, skip_row)


def _rope_tables():
    rows = SEQ // GRID_W
    row_pos = jnp.repeat(jnp.arange(rows, dtype=F32), GRID_W)
    col_pos = jnp.tile(jnp.arange(GRID_W, dtype=F32), rows)
    nf = HEAD // 4
    inv = ROPE_BASE ** (-jnp.arange(nf, dtype=F32) / nf)
    ar = row_pos[:, None] * inv[None, :]
    ac = col_pos[:, None] * inv[None, :]
    cos = jnp.concatenate([jnp.cos(ar), jnp.cos(ar), jnp.cos(ac), jnp.cos(ac)], axis=-1)
    sin = jnp.concatenate([-jnp.sin(ar), jnp.sin(ar), -jnp.sin(ac), jnp.sin(ac)], axis=-1)
    cos = jnp.tile(cos, (1, LANES // HEAD))
    sin = jnp.tile(sin, (1, LANES // HEAD))
    cos = jnp.concatenate([cos, jnp.ones((TBR, LANES), F32)], axis=0)
    sin = jnp.concatenate([sin, jnp.zeros((TBR, LANES), F32)], axis=0)
    return cos, sin


def _head_norm_rope(x, g, bd, cos, sin, scale):
    w = x.shape[1]
    xx = x * x
    hi, lo = _split(xx)
    ms = _dot(hi, bd) + _dot(lo, bd)
    y = x * lax.rsqrt(ms + EPS) * g
    rep = w // LANES
    if rep > 1:
        cos = jnp.concatenate([cos] * rep, axis=1)
        sin = jnp.concatenate([sin] * rep, axis=1)
    lane = lax.broadcasted_iota(jnp.int32, y.shape, 1)
    half = HEAD // 4
    partner = jnp.where(lane % (2 * half) < half, pltpu.roll(y, w - half, axis=1), pltpu.roll(y, half, axis=1))
    return (y * cos + partner * sin) * scale


def _qkprep_kernel(q_ref, k_ref, qg_ref, kg_ref, bdq_ref, bdk_ref, cos_ref, sin_ref, qo_ref, ko_ref):
    cos, sin = cos_ref[...], sin_ref[...]
    qo_ref[...] = _head_norm_rope(q_ref[...], qg_ref[...], bdq_ref[...], cos, sin, HEAD ** -0.5)
    ko_ref[...] = _head_norm_rope(k_ref[...], kg_ref[...], bdk_ref[...], cos, sin, 1.0)


def _block_diag_mean(w):
    i = np.arange(w)
    return jnp.asarray((i[:, None] // HEAD == i[None, :] // HEAD) / HEAD, dtype=F32).astype(BF16)


def _qkprep(p, nrows, qg, kg, cos, sin):
    tb = TBR
    lat_blocks = T_LAT // tb
    seq_blocks = SEQ // tb
    tab = pl.BlockSpec((tb, LANES), lambda i: (jnp.where(i < lat_blocks, i % seq_blocks, seq_blocks), 0))
    return pl.pallas_call(
        _qkprep_kernel,
        grid=(nrows // tb,),
        in_specs=[
            pl.BlockSpec((tb, QC), lambda i: (i, HYC // QC)),
            pl.BlockSpec((tb, KVC), lambda i: (i, (HYC + QC) // KVC)),
            pl.BlockSpec((1, QC), lambda i: (0, 0)),
            pl.BlockSpec((1, KVC), lambda i: (0, 0)),
            pl.BlockSpec((QC, QC), lambda i: (0, 0)),
            pl.BlockSpec((KVC, KVC), lambda i: (0, 0)),
            tab, tab,
        ],
        out_specs=[pl.BlockSpec((tb, QC), lambda i: (i, 0)), pl.BlockSpec((tb, KVC), lambda i: (i, 0))],
        out_shape=[jax.ShapeDtypeStruct((nrows, QC), F32), jax.ShapeDtypeStruct((nrows, KVC), F32)],
        compiler_params=_cp(("arbitrary",)),
        name="qk_norm_rope",
    )(p, p, jnp.tile(qg, NHEADS).reshape(1, QC), jnp.tile(kg, NKV).reshape(1, KVC),
      _block_diag_mean(QC), _block_diag_mean(KVC), cos, sin)


def _attn_kernel(sink_ref, q_ref, kp_ref, kc_ref, kn_ref, vp_ref, vc_ref, vn_ref, kx_ref, vx_ref, o_ref):
    i = pl.program_id(0)
    lat_blocks = T_LAT // TQ
    seq_blocks = SEQ // TQ
    is_lat = i < lat_blocks
    blk = i % seq_blocks
    kwin = jnp.concatenate([kp_ref[...], kc_ref[...], kn_ref[...]], axis=0).astype(BF16)
    vwin = jnp.concatenate([vp_ref[...], vc_ref[...], vn_ref[...]], axis=0).astype(BF16)
    kctx = kx_ref[...].astype(BF16)
    vctx = vx_ref[...].astype(BF16)
    q = q_ref[...].astype(BF16)
    rows = GROUP * TQ
    r = lax.broadcasted_iota(jnp.int32, (rows, 3 * TQ), 0) % TQ
    j = lax.broadcasted_iota(jnp.int32, (rows, 3 * TQ), 1)
    kpos = blk * TQ - TQ + j
    ok = (jnp.abs(j - TQ - r) <= WINDOW) & (kpos >= 0) & (kpos < SEQ) & is_lat
    dn = (((1,), (1,)), ((), ()))
    outs = []
    for h in range(NKV):
        cols = slice(h * HEAD, (h + 1) * HEAD)
        qs = jnp.concatenate([q[:, (GROUP * h + g) * HEAD:(GROUP * h + g + 1) * HEAD] for g in range(GROUP)], axis=0)
        s_w = lax.dot_general(qs, kwin[:, cols], dn, preferred_element_type=F32)
        s_w = jnp.where(ok, s_w, NEG)
        s_c = lax.dot_general(qs, kctx[:, cols], dn, preferred_element_type=F32)
        sk = jnp.concatenate([jnp.full((TQ, 1), sink_ref[GROUP * h + g], F32) for g in range(GROUP)], axis=0)
        m = jnp.maximum(jnp.maximum(jnp.max(s_w, axis=1, keepdims=True), jnp.max(s_c, axis=1, keepdims=True)), sk)
        p_w = jnp.exp(s_w - m)
        p_c = jnp.exp(s_c - m)
        den = jnp.sum(p_w, axis=1, keepdims=True) + jnp.sum(p_c, axis=1, keepdims=True) + jnp.exp(sk - m)
        o = (_dot(p_w.astype(BF16), vwin[:, cols]) + _dot(p_c.astype(BF16), vctx[:, cols])) / den
        outs += [o[g * TQ:(g + 1) * TQ] for g in range(GROUP)]
    o_ref[...] = jnp.concatenate(outs, axis=1)


def _attention(qn, kn, p, nrows, sink):
    nq = nrows // TQ
    lat_blocks = T_LAT // TQ
    seq_blocks = SEQ // TQ
    vcol = (HYC + QC + KVC) // KVC

    def seq_of(i):
        return jnp.where(i < lat_blocks, i // seq_blocks, (i - lat_blocks) // (CTXL // TQ))

    prev = lambda i: jnp.maximum(i - 1, 0)
    nxt = lambda i: jnp.minimum(i + 1, nq - 1)
    ctx_blk = lambda i: T_LAT // CTXL + seq_of(i)
    grid_spec = pltpu.PrefetchScalarGridSpec(
        num_scalar_prefetch=1,
        grid=(nq,),
        in_specs=[
            pl.BlockSpec((TQ, QC), lambda i, s: (i, 0)),
            pl.BlockSpec((TQ, KVC), lambda i, s: (prev(i), 0)),
            pl.BlockSpec((TQ, KVC), lambda i, s: (i, 0)),
            pl.BlockSpec((TQ, KVC), lambda i, s: (nxt(i), 0)),
            pl.BlockSpec((TQ, KVC), lambda i, s: (prev(i), vcol)),
            pl.BlockSpec((TQ, KVC), lambda i, s: (i, vcol)),
            pl.BlockSpec((TQ, KVC), lambda i, s: (nxt(i), vcol)),
            pl.BlockSpec((CTXL, KVC), lambda i, s: (ctx_blk(i), 0)),
            pl.BlockSpec((CTXL, KVC), lambda i, s: (ctx_blk(i), vcol)),
        ],
        out_specs=pl.BlockSpec((TQ, QC), lambda i, s: (i, 0)),
    )
    return pl.pallas_call(
        _attn_kernel,
        grid_spec=grid_spec,
        out_shape=jax.ShapeDtypeStruct((nrows, QC), F32),
        compiler_params=_cp(("arbitrary",)),
        name="windowed_attention",
    )(sink, qn, kn, kn, kn, p, p, p, kn, p)


def _outproj_kernel(*refs, has_ctx):
    if has_ctx:
        x_ref, hl_ref, hc_ref, at_ref, w_ref, g_ref, o_ref = refs
    else:
        x_ref, hl_ref, at_ref, w_ref, g_ref, o_ref = refs
    i = pl.program_id(0)
    grp = _group(i * TB)
    hy = hl_ref[...]
    if has_ctx:
        hy = jnp.where(i < T_LAT // TB, hy, hc_ref[...])
    mix = _dot(hy.astype(BF16), w_ref[:HY, :]) + _dot(at_ref[...].astype(BF16), w_ref[HY:, :])
    o_ref[...] = x_ref[...] + g_ref[pl.ds(grp, 1), :] * mix


def _outproj(x, nrows, hy_l, hy_c, att, w_bf, mods, l):
    has_ctx = hy_c is not None
    lat_blocks = T_LAT // TB
    in_specs = [pl.BlockSpec((TB, D), lambda i: (i, 0)),
                pl.BlockSpec((TB, HY), lambda i: (jnp.minimum(i, lat_blocks - 1), 0))]
    args = [x, hy_l]
    if has_ctx:
        in_specs.append(pl.BlockSpec((TB, HY), lambda i: (jnp.maximum(i - lat_blocks, 0), 0)))
        args.append(hy_c)
    in_specs += [pl.BlockSpec((TB, QC), lambda i: (i, 0)), pl.BlockSpec((HY + QC, D), lambda i: (0, 0)),
                 _mod_spec(l, 2)]
    args += [att, w_bf, mods]
    return pl.pallas_call(
        functools.partial(_outproj_kernel, has_ctx=has_ctx),
        grid=(nrows // TB,),
        in_specs=in_specs,
        out_specs=pl.BlockSpec((TB, D), lambda i: (i, 0)),
        out_shape=jax.ShapeDtypeStruct((nrows, D), F32),
        compiler_params=_cp(("arbitrary",)),
        name="mixer_out_proj",
    )(*args)


HALO = 16


def _conf_kernel(a_ref, ap_ref, an_ref, x_ref, dw_ref, dwb_ref, lg_ref, lb_ref, w_ref, b_ref, g_ref, o_ref):
    tb = TBR
    row0 = pl.program_id(0) * tb
    grp = _group(row0)
    slen = jnp.where(row0 < T_LAT, SEQ, CTXL)
    first = row0 % slen == 0
    last = (row0 + tb) % slen == 0
    ap = jnp.where(first, 0.0, ap_ref[...])
    an = jnp.where(last, 0.0, an_ref[...])
    ext = jnp.concatenate([ap, a_ref[...], an], axis=0)
    pad = (CONF_K - 1) // 2
    acc = jnp.zeros((tb, D), F32)
    next_ = ext.shape[0]
    for r in range(8):
        er = ext if r == 0 else pltpu.roll(ext, next_ - r, axis=0)
        for q in range((HALO + 8) // 8 + 1):
            tap = 8 * q + r - (HALO - pad)
            if 0 <= tap < CONF_K and 8 * q + r + tb <= next_:
                acc = acc + er[8 * q:8 * q + tb] * dw_ref[tap:tap + 1, :]
    a = acc + dwb_ref[...]
    mu = jnp.mean(a, axis=-1, keepdims=True)
    var = jnp.mean(jnp.square(a - mu), axis=-1, keepdims=True)
    y = (a - mu) * lax.rsqrt(var + EPS) * lg_ref[...] + lb_ref[...]
    y = y * jax.nn.sigmoid(y)
    out = _dot(y.astype(BF16), w_ref[...]) + b_ref[...]
    o_ref[...] = x_ref[...] + g_ref[pl.ds(grp, 1), :] * out


def _conformer_tail(a, x, nrows, dw_w, dw_b, ln_g, ln_b, w2_bf, b2, mods, l):
    tb = TBR
    hb = tb // HALO
    nhalo = nrows // HALO
    row = lambda v: v.reshape(1, D)
    full = lambda shape: pl.BlockSpec(shape, lambda i: (0,) * len(shape))
    return pl.pallas_call(
        _conf_kernel,
        grid=(nrows // tb,),
        in_specs=[
            pl.BlockSpec((tb, D), lambda i: (i, 0)),
            pl.BlockSpec((HALO, D), lambda i: (jnp.maximum(i * hb - 1, 0), 0)),
            pl.BlockSpec((HALO, D), lambda i: (jnp.minimum((i + 1) * hb, nhalo - 1), 0)),
            pl.BlockSpec((tb, D), lambda i: (i, 0)),
            full((CONF_K, D)), full((1, D)), full((1, D)), full((1, D)), full((D, D)), full((1, D)),
            _mod_spec(l, 2),
        ],
        out_specs=pl.BlockSpec((tb, D), lambda i: (i, 0)),
        out_shape=jax.ShapeDtypeStruct((nrows, D), F32),
        compiler_params=_cp(("arbitrary",)),
        name="conformer_tail",
    )(a, a, a, x, dw_w, row(dw_b), row(ln_g), row(ln_b), w2_bf, row(b2), mods)


def _router_kernel(x_ref, g_ref, sh_ref, sc_ref, rw_ref, rb_ref, tri_ref, h_ref, route_ref, cnt_ref, carry_ref):
    i = pl.program_id(0)

    @pl.when(i == 0)
    def _():
        carry_ref[...] = jnp.zeros_like(carry_ref)

    grp = _group(i * TBR)
    h = _norm_mod(x_ref[...], g_ref[...], sh_ref[pl.ds(grp, 1), :], sc_ref[pl.ds(grp, 1), :])
    h_ref[...] = h
    lg = _dot3(h, rw_ref[...]) + rb_ref[...]
    lane = lax.broadcasted_iota(jnp.int32, lg.shape, 1)
    vals, idxs, hots = [], [], []
    for _ in range(TOPK):
        m = jnp.max(lg, axis=1, keepdims=True)
        idx = jnp.min(jnp.where(lg == m, lane, LANES), axis=1, keepdims=True)
        hot = lane == idx
        vals.append(m)
        idxs.append(idx)
        hots.append(hot)
        lg = jnp.where(hot, -3e38, lg)
    ex = [jnp.exp(v - vals[0]) for v in vals]
    den = ex[0] + ex[1] + ex[2] + ex[3]
    msum = jnp.zeros(lg.shape, F32)
    for hot in hots:
        msum = msum + jnp.where(hot, 1.0, 0.0)
    base = _dot(tri_ref[...], msum.astype(BF16)) + carry_ref[0:1, :]
    route = jnp.zeros(lg.shape, F32)
    for k in range(TOPK):
        pos = jnp.sum(jnp.where(hots[k], base, 0.0), axis=1, keepdims=True)
        route = route + jnp.where(lane == k, idxs[k].astype(F32), 0.0)
        route = route + jnp.where(lane == TOPK + k, ex[k] / den, 0.0)
        route = route + jnp.where(lane == 2 * TOPK + k, pos, 0.0)
    route_ref[...] = route
    carry_ref[0:1, :] = carry_ref[0:1, :] + jnp.sum(msum, axis=0, keepdims=True)
    cnt_ref[...] = carry_ref[...]


def _router(x, nrows, g, mods, l, rw, rb):
    tb = TBR
    rwp = jnp.pad(rw, ((0, 0), (0, LANES - NE)))
    rbp = jnp.concatenate([rb, jnp.full((LANES - NE,), NEG, F32)]).reshape(1, LANES)
    r = np.arange(tb)
    tri = jnp.asarray(r[None, :] < r[:, None], dtype=F32).astype(BF16)
    return pl.pallas_call(
        _router_kernel,
        grid=(nrows // tb,),
        in_specs=[
            pl.BlockSpec((tb, D), lambda i: (i, 0)),
            pl.BlockSpec((1, D), lambda i: (0, 0)),
            _mod_spec(l, 3), _mod_spec(l, 4),
            pl.BlockSpec((D, LANES), lambda i: (0, 0)),
            pl.BlockSpec((1, LANES), lambda i: (0, 0)),
            pl.BlockSpec((tb, tb), lambda i: (0, 0)),
        ],
        out_specs=[
            pl.BlockSpec((tb, D), lambda i: (i, 0)),
            pl.BlockSpec((tb, LANES), lambda i: (i, 0)),
            pl.BlockSpec((8, LANES), lambda i: (0, 0)),
        ],
        out_shape=[
            jax.ShapeDtypeStruct((nrows, D), F32),
            jax.ShapeDtypeStruct((nrows, LANES), F32),
            jax.ShapeDtypeStruct((8, LANES), F32),
        ],
        scratch_shapes=[pltpu.VMEM((8, LANES), F32)],
        compiler_params=_cp(("arbitrary",)),
        name="moe_router",
    )(x, g.reshape(1, D), mods, mods, rwp, rbp, tri)


def _row_copy(src_ref, s, dst_ref, d, sem):
    return pltpu.make_async_copy(src_ref.at[pl.ds(s, 1)], dst_ref.at[pl.ds(d, 1)], sem)


def _dispatch_kernel(dest_ref, h_ref, rows_in_ref, rows_ref, sem):
    del rows_in_ref
    base = pl.program_id(0) * TBR

    def issue(t, carry):
        for k in range(TOPK):
            _row_copy(h_ref, t, rows_ref, dest_ref[(base + t) * TOPK + k], sem).start()
        return carry

    lax.fori_loop(0, TBR, issue, 0, unroll=4)
    for k in range(TOPK):
        pltpu.make_async_copy(h_ref, rows_ref.at[pl.ds(0, TBR)], sem).wait()


def _dispatch(dest, h, nrows, rows0):
    grid_spec = pltpu.PrefetchScalarGridSpec(
        num_scalar_prefetch=1,
        grid=(nrows // TBR,),
        in_specs=[pl.BlockSpec((TBR, D), lambda i, d: (i, 0)), pl.BlockSpec(memory_space=pl.ANY)],
        out_specs=pl.BlockSpec(memory_space=pl.ANY),
        scratch_shapes=[pltpu.SemaphoreType.DMA(())],
    )
    return pl.pallas_call(
        _dispatch_kernel,
        grid_spec=grid_spec,
        out_shape=jax.ShapeDtypeStruct(rows0.shape, F32),
        input_output_aliases={2: 0},
        compiler_params=_cp(("arbitrary",)),
        name="moe_dispatch",
    )(dest, h, rows0)


def _expert_kernel(be_ref, nu_ref, x_ref, wg_ref, bg_ref, wu_ref, bu_ref, wd_ref, bd_ref, o_ref,
                   wgb_ref, wub_ref, wdb_ref):
    j = pl.program_id(0)
    e = be_ref[j]
    prev = be_ref[jnp.maximum(j - 1, 0)]

    @pl.when((j == 0) | (e != prev))
    def _():
        wgb_ref[...] = wg_ref[...].astype(BF16)
        wub_ref[...] = wu_ref[...].astype(BF16)
        wdb_ref[...] = wd_ref[...].astype(BF16)

    @pl.when(j < nu_ref[0])
    def _():
        xb = x_ref[...].astype(BF16)
        g = jnp.minimum(_dot(xb, wgb_ref[...]) + bg_ref[...], SWIGLU_LIMIT)
        u = jnp.clip(_dot(xb, wub_ref[...]) + bu_ref[...], -SWIGLU_LIMIT, SWIGLU_LIMIT)
        a = g * jax.nn.sigmoid(SWIGLU_ALPHA * g) * (u + 1.0)
        o_ref[...] = _dot(a.astype(BF16), wdb_ref[...]) + bd_ref[...]

    @pl.when(j >= nu_ref[0])
    def _():
        o_ref[...] = jnp.zeros_like(o_ref)


def _experts(blk_exp, nused, rows, l, wg, bg, wu, bu, wd, bd):
    nrows = rows.shape[0]
    wspec = pl.BlockSpec((None, None, D, D), lambda j, be, nu: (l, be[j], 0, 0))
    bspec = pl.BlockSpec((None, None, 1, D), lambda j, be, nu: (l, be[j], 0, 0))
    grid_spec = pltpu.PrefetchScalarGridSpec(
        num_scalar_prefetch=2,
        grid=(nrows // BM,),
        in_specs=[pl.BlockSpec((BM, D), lambda j, be, nu: (jnp.minimum(j, nu[0] - 1), 0)),
                  wspec, bspec, wspec, bspec, wspec, bspec],
        out_specs=pl.BlockSpec((BM, D), lambda j, be, nu: (j, 0)),
        scratch_shapes=[pltpu.VMEM((D, D), BF16)] * 3,
    )
    b3 = lambda b: b.reshape(DEPTH, NE, 1, D)
    return pl.pallas_call(
        _expert_kernel,
        grid_spec=grid_spec,
        out_shape=jax.ShapeDtypeStruct((nrows, D), F32),
        compiler_params=_cp(("arbitrary",)),
        name="moe_experts",
    )(blk_exp, nused, rows, wg, b3(bg), wu, b3(bu), wd, b3(bd))


def _combine_kernel(dest_ref, x_ref, route_ref, g_ref, y_ref, o_ref, buf_ref, sem):
    i = pl.program_id(0)
    slot = i % 2
    grp = _group(i * TBC)

    def gather(blk, s):
        def issue(t, carry):
            for k in range(TOPK):
                _row_copy(y_ref, dest_ref[(blk * TBC + t) * TOPK + k], buf_ref.at[s, k], t, sem.at[s]).start()
            return carry
        lax.fori_loop(0, TBC, issue, 0, unroll=4)

    @pl.when(i == 0)
    def _():
        gather(0, 0)

    @pl.when(i + 1 < pl.num_programs(0))
    def _():
        gather(i + 1, 1 - slot)

    for k in range(TOPK):
        pltpu.make_async_copy(y_ref.at[pl.ds(0, TBC)], buf_ref.at[slot, k], sem.at[slot]).wait()
    acc = jnp.zeros((TBC, D), F32)
    for k in range(TOPK):
        acc = acc + route_ref[:, TOPK + k:TOPK + k + 1] * buf_ref[slot, k]
    o_ref[...] = x_ref[...] + g_ref[pl.ds(grp, 1), :] * acc


def _combine(dest, x, nrows, route, mods, l, y_rows):
    grid_spec = pltpu.PrefetchScalarGridSpec(
        num_scalar_prefetch=1,
        grid=(nrows // TBC,),
        in_specs=[pl.BlockSpec((TBC, D), lambda i, d: (i, 0)),
                  pl.BlockSpec((TBC, LANES), lambda i, d: (i, 0)),
                  _mod_spec(l, 5),
                  pl.BlockSpec(memory_space=pl.ANY)],
        out_specs=pl.BlockSpec((TBC, D), lambda i, d: (i, 0)),
        scratch_shapes=[pltpu.VMEM((2, TOPK, TBC, D), F32), pltpu.SemaphoreType.DMA((2,))],
    )
    return pl.pallas_call(
        _combine_kernel,
        grid_spec=grid_spec,
        out_shape=jax.ShapeDtypeStruct((nrows, D), F32),
        compiler_params=_cp(("arbitrary",)),
        name="moe_combine",
    )(dest, x, route, mods, y_rows)


def _moe(x, nrows, g, mods, l, rw, rb, wg, bg, wu, bu, wd, bd):
    h, route, cnt = _router(x, nrows, g, mods, l, rw, rb)
    counts = cnt[0, :NE].astype(jnp.int32)
    pcounts = (counts + BM - 1) // BM * BM
    pends = jnp.cumsum(pcounts)
    pstarts = pends - pcounts
    top_i = route[:, :TOPK].astype(jnp.int32)
    pos = route[:, 2 * TOPK:3 * TOPK].astype(jnp.int32)
    start_of = jnp.sum(jnp.where(top_i[..., None] == jnp.arange(NE), pstarts, 0), axis=-1)
    dest = (start_of + pos).reshape(-1)
    n_rows = nrows * TOPK + NE * BM
    n_blk = n_rows // BM
    nused = pends[-1] // BM
    blk = jnp.minimum(jnp.arange(n_blk), nused - 1) * BM
    blk_exp = jnp.minimum(jnp.sum(blk[:, None] >= pends[None, :], axis=1), NE - 1).astype(jnp.int32)
    rows = _dispatch(dest, h, nrows, jnp.zeros((n_rows, D), F32))
    y_rows = _experts(blk_exp, nused.reshape(1).astype(jnp.int32), rows, l, wg, bg, wu, bu, wd, bd)
    return _combine(dest, x, nrows, route, mods, l, y_rows)


def _hyena(u, hr, hi, skip, tabs, n1, n2):
    nc = HY // 128
    br, bi = _fft_a(u, 0, tabs, n1, n2)
    dr, di = _fft_b(br, bi, hr, hi, 0, tabs, n1, n2)
    z1 = _fft_c(dr, di, u, 0, u, nc, skip[0:1], tabs, n1, n2)
    br, bi = _fft_a(z1, 0, tabs, n1, n2)
    dr, di = _fft_b(br, bi, hr, hi, HY // 256, tabs, n1, n2)
    return _fft_c(dr, di, z1, 0, u, 2 * nc, skip[1:2], tabs, n1, n2)


def _filter_spectra(ls, filt, tabs, n1, n2):
    buf = _filters(ls, *filt)
    fr, fi = _fft_a_filter(buf, tabs, n1, n2)
    return _fft_b_filter(fr, fi, tabs, n1, n2)


FFT_LAT = (128, 64)
FFT_CTX = (32, 16)


def kernel(x, c, ctx, c_ctx, mod_w, mod_b, norm1_g, norm2_g, ev_w_in, ev_w_out, hy_conv_w, hy_conv_b, hy_w1, hy_b1, hy_w2, hy_b2, hy_w3, hy_b3, hy_w_out, hy_freq, hy_skip, q_norm_g, k_norm_g, attn_sink, cf_w1, cf_b1, cf_dw_w, cf_dw_b, cf_ln_g, cf_ln_b, cf_w2, cf_b2, moe_router_w, moe_router_b, moe_w_gate, moe_b_gate, moe_w_up, moe_b_up, moe_w_down, moe_b_down):
    cc8 = jnp.concatenate([c, c_ctx[None, :], jnp.zeros((8 - NB - 1, D), F32)], axis=0)
    mods = _adaln(cc8, mod_w, mod_b)
    xs = jnp.concatenate([x.reshape(T_LAT, D), ctx.reshape(T_CTX, D)], axis=0)
    cos, sin = _rope_tables()
    tabs_l = _fft_tables(*FFT_LAT)
    tabs_c = _fft_tables(*FFT_CTX)
    nrows = T_ALL
    for l in range(DEPTH):
        ctx_live = l < (DEPTH - 1) // 2 * 2
        if l % 2 == 0:
            e = l // 2
            filt = (hy_w1[e], hy_b1[e], hy_w2[e], hy_b2[e], hy_w3[e], hy_b3[e], hy_w_out[e], hy_freq[e])
            p = _nmm(xs, T_ALL, norm1_g[l], mods, l, ev_w_in[e].astype(BF16))
            hr, hi = _filter_spectra(SEQ, filt, tabs_l, *FFT_LAT)
            u = _sconv(p, SEQ, 0, hy_conv_w[e], hy_conv_b[e])
            hy_l = _hyena(u, hr, hi, hy_skip[e], tabs_l, *FFT_LAT)
            hy_c = None
            if ctx_live:
                hrc, hic = _filter_spectra(CTXL, filt, tabs_c, *FFT_CTX)
                uc = _sconv(p, CTXL, T_LAT // CTXL, hy_conv_w[e], hy_conv_b[e])
                hy_c = _hyena(uc, hrc, hic, hy_skip[e], tabs_c, *FFT_CTX)
            qn, kn = _qkprep(p, T_ALL, q_norm_g[e], k_norm_g[e], cos, sin)
            nrows = T_ALL if ctx_live else T_LAT
            att = _attention(qn, kn, p, nrows, attn_sink[e])
            xs = _outproj(xs, nrows, hy_l, hy_c, att, ev_w_out[e].astype(BF16), mods, l)
        else:
            o = l // 2
            a = _nmm(xs, nrows, norm1_g[l], mods, l, cf_w1[o].astype(BF16), cf_b1[o])
            xs = _conformer_tail(a, xs, nrows, cf_dw_w[o], cf_dw_b[o], cf_ln_g[o], cf_ln_b[o],
                                 cf_w2[o].astype(BF16), cf_b2[o], mods, l)
        xs = _moe(xs, nrows, norm2_g[l], mods, l, moe_router_w[l], moe_router_b[l], moe_w_gate, moe_b_gate,
                  moe_w_up, moe_b_up, moe_w_down, moe_b_down)
    return xs[:T_LAT].reshape(NB, SEQ, D)
```
